```python
import math
import jax, jax.numpy as jnp
from jax import lax
import numpy as np


D_MODEL = 2048
BATCH = 16
SEQ = 2048
DEPTH = 2

HEAD_DIM = 128
MIX_WIDTH = D_MODEL // 2
N_BRANCH = 3
MOBA_HEADS = MIX_WIDTH // HEAD_DIM
MOBA_BLOCK = 256
MOBA_TOPK = 3
MOBA_Q_BLOCK = 64
DIFF_HEADS = MIX_WIDTH // (2 * HEAD_DIM)
SB_HEADS = MIX_WIDTH // HEAD_DIM
ATT_Q_BLOCK = 128
IN_COLS = 9 * MIX_WIDTH + N_BRANCH * D_MODEL
N_GROUPS = 4
EXPERTS_PER_GROUP = 8
N_EXPERTS = N_GROUPS * EXPERTS_PER_GROUP
EXPERT_FF = D_MODEL // 4
EXPERT_TOPK = 2
MOE_BLOCK = 256
RMS_EPS = 1e-6

kernel_name = 'hybrid_moba_diff_stickbreak_hmoe'


def rms_norm(x, g):
    xf = x.astype(jnp.float32)
    y = xf * lax.rsqrt(jnp.mean(xf * xf, axis=-1, keepdims=True) + RMS_EPS)
    return (y * g.astype(jnp.float32)).astype(x.dtype)


def alibi_slopes(n_heads):
    return jnp.exp2(-8.0 * jnp.arange(1, n_heads + 1, dtype=jnp.float32) / n_heads)


def split_heads(x, n_heads):
    B, S, _ = x.shape
    return x.reshape(B, S, n_heads, -1).transpose(0, 2, 1, 3)


def merge_heads(x):
    B, H, S, Dh = x.shape
    return x.transpose(0, 2, 1, 3).reshape(B, S, H * Dh)


def moba_sequence(q, k, v, slopes):
    H, S, Dh = q.shape
    scale = Dh ** -0.5
    nb = -(-S // MOBA_BLOCK)
    pad = nb * MOBA_BLOCK - S
    kp = jnp.pad(k, ((0, 0), (0, pad), (0, 0)))
    vp = jnp.pad(v, ((0, 0), (0, pad), (0, 0)))
    kb = kp.reshape(H, nb, MOBA_BLOCK, Dh)
    vb = vp.reshape(H, nb, MOBA_BLOCK, Dh)
    kmean = jnp.mean(kb.astype(jnp.float32), axis=2)
    topk = min(MOBA_TOPK, nb)
    nq = S // MOBA_Q_BLOCK
    qs = q.reshape(H, nq, MOBA_Q_BLOCK, Dh).swapaxes(0, 1)
    gather = jax.vmap(lambda blocks, idx: blocks[idx])
    r = jnp.arange(MOBA_BLOCK)

    def one_block(args):
        qi, qblk = args
        t = qi * MOBA_Q_BLOCK + jnp.arange(MOBA_Q_BLOCK)
        own = (qi * MOBA_Q_BLOCK) // MOBA_BLOCK
        gate = jnp.einsum('hqd,hnd->hqn', qblk.astype(jnp.float32), kmean)
        gate = jnp.where(jnp.arange(nb) < own, gate, -jnp.inf)
        _, sel = lax.top_k(gate, topk)
        valid = sel < own
        ksel = gather(kb, sel)
        vsel = gather(vb, sel)
        s_sel = sel[..., None] * MOBA_BLOCK + r
        lg_sel = jnp.einsum('hqd,hqkrd->hqkr', qblk, ksel).astype(jnp.float32) * scale
        lg_sel = lg_sel - slopes[:, None, None, None] * (t[None, :, None, None] - s_sel)
        lg_sel = jnp.where(valid[..., None], lg_sel, -jnp.inf).reshape(H, MOBA_Q_BLOCK, topk * MOBA_BLOCK)
        kown = lax.dynamic_slice_in_dim(kp, own * MOBA_BLOCK, MOBA_BLOCK, axis=1)
        vown = lax.dynamic_slice_in_dim(vp, own * MOBA_BLOCK, MOBA_BLOCK, axis=1)
        s_own = own * MOBA_BLOCK + r
        lg_own = jnp.einsum('hqd,hrd->hqr', qblk, kown).astype(jnp.float32) * scale
        lg_own = lg_own - slopes[:, None, None] * (t[:, None] - s_own[None, :])
        lg_own = jnp.where(s_own[None, :] <= t[:, None], lg_own, -jnp.inf)
        p = jax.nn.softmax(jnp.concatenate([lg_sel, lg_own], axis=-1), axis=-1).astype(v.dtype)
        p_sel = p[..., :topk * MOBA_BLOCK].reshape(H, MOBA_Q_BLOCK, topk, MOBA_BLOCK)
        p_own = p[..., topk * MOBA_BLOCK:]
        return (jnp.einsum('hqkr,hqkrd->hqd', p_sel, vsel)
                + jnp.einsum('hqr,hrd->hqd', p_own, vown))

    out = lax.map(one_block, (jnp.arange(nq), qs))
    return out.swapaxes(0, 1).reshape(H, S, Dh)


def diff_attention(q, k, v, lam, slopes):
    S = q.shape[3]
    scale = q.shape[-1] ** -0.5
    outs = []
    for q0 in range(0, S, ATT_Q_BLOCK):
        end = q0 + ATT_Q_BLOCK
        t = jnp.arange(q0, end)
        s = jnp.arange(end)
        lg = jnp.einsum('bhmqd,bhmkd->bhmqk', q[:, :, :, q0:end], k[:, :, :, :end]).astype(jnp.float32) * scale
        lg = lg - slopes[:, None, None, None] * (t[:, None] - s[None, :])
        lg = jnp.where(s[None, :] <= t[:, None], lg, -jnp.inf)
        p = jax.nn.softmax(lg, axis=-1)
        w = p[:, :, 0] - lam * p[:, :, 1]
        outs.append(jnp.einsum('bhqk,bhkv->bhqv', w.astype(v.dtype), v[:, :, :end]))
    return jnp.concatenate(outs, axis=2)


def stick_breaking_attention(q, k, v):
    S = q.shape[2]
    scale = q.shape[-1] ** -0.5
    outs = []
    for q0 in range(0, S, ATT_Q_BLOCK):
        end = q0 + ATT_Q_BLOCK
        t = jnp.arange(q0, end)
        s = jnp.arange(end)
        strict = s[None, :] < t[:, None]
        z = jnp.einsum('bhqd,bhkd->bhqk', q[:, :, q0:end], k[:, :, :end]).astype(jnp.float32) * scale
        log_keep = jnp.where(strict, jax.nn.log_sigmoid(-z), 0.0)
        after = lax.cumsum(log_keep, axis=3, reverse=True) - log_keep
        a = jnp.where(strict, jnp.exp(jax.nn.log_sigmoid(z) + after), 0.0)
        outs.append(jnp.einsum('bhqk,bhkd->bhqd', a.astype(v.dtype), v[:, :, :end]))
    return jnp.concatenate(outs, axis=2)


def hier_moe(h, w_rg, w_re, w_g, w_u, w_d):
    T, D = h.shape
    g_prob = jax.nn.softmax((h @ w_rg).astype(jnp.float32), axis=-1)
    g_top, g_idx = lax.top_k(g_prob, 1)
    e_logits = (h @ w_re).astype(jnp.float32).reshape(T, N_GROUPS, EXPERTS_PER_GROUP)
    e_sel = e_logits[jnp.arange(T), g_idx[:, 0]]
    e_top, e_local = lax.top_k(jax.nn.softmax(e_sel, axis=-1), EXPERT_TOPK)
    weights = g_top * e_top / jnp.sum(e_top, axis=-1, keepdims=True)
    expert = g_idx * EXPERTS_PER_GROUP + e_local
    n_assign = T * EXPERT_TOPK
    flat_e = expert.reshape(-1)
    flat_tok = jnp.arange(n_assign) // EXPERT_TOPK
    flat_w = weights.reshape(-1)
    order = jnp.argsort(flat_e)
    se = flat_e[order]
    tok_sorted = flat_tok[order]
    counts = jnp.bincount(flat_e, length=N_EXPERTS)
    padded = ((counts + MOE_BLOCK - 1) // MOE_BLOCK) * MOE_BLOCK
    pend = jnp.cumsum(padded)
    pstart = pend - padded
    cstart = jnp.cumsum(counts) - counts
    dest = pstart[se] + (jnp.arange(n_assign) - cstart[se])
    n_blocks = -(-n_assign // MOE_BLOCK) + N_EXPERTS
    xs = jnp.zeros((n_blocks * MOE_BLOCK, D), h.dtype).at[dest].set(h[tok_sorted])
    block_expert = jnp.minimum(jnp.searchsorted(pend, jnp.arange(n_blocks) * MOE_BLOCK, side='right'), N_EXPERTS - 1)

    def expert_block(args):
        xb, e = args
        return (jax.nn.silu(xb @ w_g[e]) * (xb @ w_u[e])) @ w_d[e]

    ys = lax.map(expert_block, (xs.reshape(n_blocks, MOE_BLOCK, D), block_expert)).reshape(-1, D)
    contrib = flat_w[order][:, None] * ys[dest].astype(jnp.float32)
    out = jnp.zeros((T, D), jnp.float32).at[tok_sorted].add(contrib)
    return out.astype(h.dtype)


def setup_inputs(seed: int = 0) -> dict:
    key = jax.random.key(seed)
    ks = jax.random.split(key, 22)
    L, D = DEPTH, D_MODEL

    def nrm(k, shape, scale):
        return jax.random.normal(k, shape, jnp.float32) * scale

    return {
        'x': nrm(ks[0], (BATCH, SEQ, D), 1.0),
        'attn_norm_g': 1.0 + nrm(ks[1], (L, D), 0.02),
        'w_in': nrm(ks[2], (L, D, IN_COLS), D ** -0.5),
        'gate_bias': nrm(ks[3], (L, N_BRANCH, D), 0.02),
        'moba_q_gain': 1.0 + nrm(ks[4], (L, HEAD_DIM), 0.02),
        'moba_k_gain': 1.0 + nrm(ks[5], (L, HEAD_DIM), 0.02),
        'diff_q_gain': 1.0 + nrm(ks[6], (L, HEAD_DIM), 0.02),
        'diff_k_gain': 1.0 + nrm(ks[7], (L, HEAD_DIM), 0.02),
        'diff_lambda_q1': nrm(ks[8], (L, HEAD_DIM), 0.1),
        'diff_lambda_k1': nrm(ks[9], (L, HEAD_DIM), 0.1),
        'diff_lambda_q2': nrm(ks[10], (L, HEAD_DIM), 0.1),
        'diff_lambda_k2': nrm(ks[11], (L, HEAD_DIM), 0.1),
        'diff_head_gain': 1.0 + nrm(ks[12], (L, 2 * HEAD_DIM), 0.02),
        'w_branch': nrm(ks[13], (L, N_BRANCH, MIX_WIDTH, D), MIX_WIDTH ** -0.5),
        'w_out': nrm(ks[14], (L, D, D), D ** -0.5),
        'ffn_norm_g': 1.0 + nrm(ks[15], (L, D), 0.02),
        'w_router_group': nrm(ks[16], (L, D, N_GROUPS), D ** -0.5),
        'w_router_expert': nrm(ks[17], (L, D, N_GROUPS * EXPERTS_PER_GROUP), D ** -0.5),
        'w_expert_gate': nrm(ks[18], (L, N_EXPERTS, D, EXPERT_FF), D ** -0.5),
        'w_expert_up': nrm(ks[19], (L, N_EXPERTS, D, EXPERT_FF), D ** -0.5),
        'w_expert_down': nrm(ks[20], (L, N_EXPERTS, EXPERT_FF, D), EXPERT_FF ** -0.5),
    }


def reference(x, attn_norm_g, w_in, gate_bias, moba_q_gain, moba_k_gain, diff_q_gain, diff_k_gain,
              diff_lambda_q1, diff_lambda_k1, diff_lambda_q2, diff_lambda_k2, diff_head_gain,
              w_branch, w_out, ffn_norm_g, w_router_group, w_router_expert,
              w_expert_gate, w_expert_up, w_expert_down):
    B, S, D = x.shape
    slopes_moba = alibi_slopes(MOBA_HEADS)
    slopes_diff = alibi_slopes(DIFF_HEADS)
    for l in range(DEPTH):
        h = rms_norm(x, attn_norm_g[l])
        proj = h @ w_in[l]
        mix = proj[..., :9 * MIX_WIDTH].reshape(B, S, N_BRANCH, 3, MIX_WIDTH)
        gate_logits = proj[..., 9 * MIX_WIDTH:].reshape(B, S, N_BRANCH, D)

        qa = rms_norm(split_heads(mix[:, :, 0, 0], MOBA_HEADS), moba_q_gain[l])
        ka = rms_norm(split_heads(mix[:, :, 0, 1], MOBA_HEADS), moba_k_gain[l])
        va = split_heads(mix[:, :, 0, 2], MOBA_HEADS)
        oa = lax.map(lambda qkv: moba_sequence(qkv[0], qkv[1], qkv[2], slopes_moba), (qa, ka, va))

        qb = rms_norm(mix[:, :, 1, 0].reshape(B, S, DIFF_HEADS, 2, HEAD_DIM).transpose(0, 2, 3, 1, 4), diff_q_gain[l])
        kb = rms_norm(mix[:, :, 1, 1].reshape(B, S, DIFF_HEADS, 2, HEAD_DIM).transpose(0, 2, 3, 1, 4), diff_k_gain[l])
        vb = split_heads(mix[:, :, 1, 2], DIFF_HEADS)
        lam_init = 0.8 - 0.6 * math.exp(-0.3 * l)
        lam = (jnp.exp(jnp.sum(diff_lambda_q1[l].astype(jnp.float32) * diff_lambda_k1[l].astype(jnp.float32)))
               - jnp.exp(jnp.sum(diff_lambda_q2[l].astype(jnp.float32) * diff_lambda_k2[l].astype(jnp.float32)))
               + lam_init)
        ob = diff_attention(qb, kb, vb, lam, slopes_diff)
        ob = rms_norm(ob, diff_head_gain[l]) * (1.0 - lam_init)

        qc = split_heads(mix[:, :, 2, 0], SB_HEADS)
        kc = split_heads(mix[:, :, 2, 1], SB_HEADS)
        vc = split_heads(mix[:, :, 2, 2], SB_HEADS)
        oc = stick_breaking_attention(qc, kc, vc)

        branches = jnp.stack([merge_heads(oa), merge_heads(ob), merge_heads(oc)], axis=2)
        branch_out = jnp.einsum('bsgc,gcd->bsgd', branches, w_branch[l])
        gates = jax.nn.sigmoid(gate_logits + gate_bias[l])
        mixed = jnp.sum(gates * branch_out, axis=2)
        x = x + mixed @ w_out[l]

        h = rms_norm(x, ffn_norm_g[l])
        y = hier_moe(h.reshape(B * S, D), w_router_group[l], w_router_expert[l],
                     w_expert_gate[l], w_expert_up[l], w_expert_down[l])
        x = x + y.reshape(B, S, D)
    return x
```

```python
import functools
import math

import jax
import jax.numpy as jnp
from jax import lax
from jax.experimental import pallas as pl
from jax.experimental.pallas import tpu as pltpu

F32 = jnp.float32
BF16 = jnp.bfloat16

HEAD_DIM = 128
N_BRANCH = 3
MOBA_BLOCK = 256
MOBA_TOPK = 3
N_GROUPS = 4
EXPERTS_PER_GROUP = 8
N_EXPERTS = N_GROUPS * EXPERTS_PER_GROUP
EXPERT_TOPK = 2
MOE_BLOCK = 256
RMS_EPS = 1e-6
ATT_BLOCK = 256
LANES = 128
NEG_INF = float("-inf")

KIND_PLAIN, KIND_NORM, KIND_GATE = 0, 1, 2


def _nt_dot(a, b):
    return lax.dot_general(a, b, (((1,), (1,)), ((), ())), preferred_element_type=F32)


def _dot(a, b):
    return jnp.dot(a, b, preferred_element_type=F32)


def _split_bf16(x):
    hi = x.astype(BF16)
    lo = (x - hi.astype(F32)).astype(BF16)
    return hi, lo


def _in_proj_kernel(kind_ref, x_ref, g_ref, w_ref, gain_ref, bias_ref, o_ref, xn_ref):
    j = pl.program_id(1)

    @pl.when(j == 0)
    def _():
        x = x_ref[...]
        ms = jnp.mean(x * x, axis=-1, keepdims=True)
        xn_ref[...] = (x * lax.rsqrt(ms + RMS_EPS) * g_ref[...]).astype(BF16)

    acc = _dot(xn_ref[...], w_ref[...])
    kind = kind_ref[j]

    @pl.when(kind == KIND_PLAIN)
    def _():
        o_ref[...] = acc.astype(o_ref.dtype)

    @pl.when(kind == KIND_NORM)
    def _():
        for c in range(acc.shape[1] // HEAD_DIM):
            sl = slice(c * HEAD_DIM, (c + 1) * HEAD_DIM)
            blk = acc[:, sl]
            ms = jnp.mean(blk * blk, axis=-1, keepdims=True)
            o_ref[:, sl] = (blk * lax.rsqrt(ms + RMS_EPS) * gain_ref[:, sl]).astype(o_ref.dtype)

    @pl.when(kind == KIND_GATE)
    def _():
        z = acc + bias_ref[...]
        o_ref[...] = (1.0 / (1.0 + jnp.exp(-z))).astype(o_ref.dtype)


def _in_proj(x2d, g, w, gain_cols, bias_cols, kinds, *, tm, tn):
    t, d = x2d.shape
    n = w.shape[1]
    grid = (t // tm, n // tn)
    return pl.pallas_call(
        _in_proj_kernel,
        out_shape=jax.ShapeDtypeStruct((t, n), BF16),
        grid_spec=pltpu.PrefetchScalarGridSpec(
            num_scalar_prefetch=1,
            grid=grid,
            in_specs=[
                pl.BlockSpec((tm, d), lambda i, j, k: (i, 0)),
                pl.BlockSpec((1, d), lambda i, j, k: (0, 0)),
                pl.BlockSpec((d, tn), lambda i, j, k: (0, j)),
                pl.BlockSpec((1, tn), lambda i, j, k: (0, j)),
                pl.BlockSpec((1, tn), lambda i, j, k: (0, j)),
            ],
            out_specs=pl.BlockSpec((tm, tn), lambda i, j, k: (i, j)),
            scratch_shapes=[pltpu.VMEM((tm, d), BF16)],
        ),
        compiler_params=pltpu.CompilerParams(
            dimension_semantics=("parallel", "arbitrary"),
            vmem_limit_bytes=48 * 1024 * 1024),
        name="in_proj",
    )(kinds, x2d, g, w, gain_cols, bias_cols)


def _moba_kernel(slopes_ref, q_ref, k_ref, v_ref, o_ref, km_ref, m_ref, l_ref, acc_ref,
                 *, nb, blk, topk, scale):
    h = pl.program_id(1)
    qi = pl.program_id(2)

    @pl.when(qi == 0)
    def _():
        km_ref[...] = jnp.zeros_like(km_ref)
        for n in range(nb):
            kblk = k_ref[n * blk:(n + 1) * blk, :].astype(F32)
            km_ref[n:n + 1, :] = jnp.sum(kblk, axis=0, keepdims=True) * (1.0 / blk)

    q = q_ref[...]
    km_hi, km_lo = _split_bf16(km_ref[...])
    gate = _nt_dot(q, km_hi) + _nt_dot(q, km_lo)
    lane = lax.broadcasted_iota(jnp.int32, gate.shape, 1)
    lane_f = lane.astype(F32)
    gate = jnp.where(lane < qi, gate, NEG_INF)
    sel = jnp.zeros(gate.shape, F32)
    for _ in range(topk):
        gmax = jnp.max(gate, axis=-1, keepdims=True)
        first = jnp.min(jnp.where(gate == gmax, lane_f, float(LANES)), axis=-1, keepdims=True)
        pick = jnp.logical_and(lane_f == first, gmax > NEG_INF)
        sel = jnp.where(pick, 1.0, sel)
        gate = jnp.where(pick, NEG_INF, gate)

    slope = slopes_ref[h]
    row = lax.broadcasted_iota(jnp.int32, (blk, blk), 0)
    col = lax.broadcasted_iota(jnp.int32, (blk, blk), 1)
    nslope_rel = (row - col).astype(F32) * (-slope)

    own = pl.multiple_of(qi * blk, blk)
    s = _nt_dot(q, k_ref[pl.ds(own, blk), :]) * scale + nslope_rel
    s = jnp.where(col <= row, s, NEG_INF)
    m0 = jnp.max(s, axis=-1, keepdims=True)
    p = jnp.exp(s - m0)
    m_ref[...] = m0
    l_ref[...] = jnp.sum(p, axis=-1, keepdims=True)
    acc_ref[...] = _dot(p.astype(BF16), v_ref[pl.ds(own, blk), :])

    def past_block(n, carry):
        start = pl.multiple_of(n * blk, blk)
        dist = ((qi - n) * blk).astype(F32)
        s = _nt_dot(q, k_ref[pl.ds(start, blk), :]) * scale + (nslope_rel - slope * dist)
        chosen = jnp.sum(jnp.where(lane == n, sel, 0.0), axis=-1, keepdims=True) > 0.5
        s = jnp.where(chosen, s, NEG_INF)
        m_old = m_ref[...]
        m_new = jnp.maximum(m_old, jnp.max(s, axis=-1, keepdims=True))
        alpha = jnp.exp(m_old - m_new)
        p = jnp.exp(s - m_new)
        m_ref[...] = m_new
        l_ref[...] = alpha * l_ref[...] + jnp.sum(p, axis=-1, keepdims=True)
        acc_ref[...] = alpha * acc_ref[...] + _dot(p.astype(BF16), v_ref[pl.ds(start, blk), :])
        return carry

    lax.fori_loop(0, qi, past_block, 0)
    o_ref[...] = (acc_ref[...] / l_ref[...]).astype(o_ref.dtype)


def _moba(proj, slopes, *, n_heads, q_col, k_col, v_col):
    b, s, _ = proj.shape
    blk = MOBA_BLOCK
    nb = s // blk
    assert s % blk == 0 and nb <= LANES
    kern = functools.partial(_moba_kernel, nb=nb, blk=blk, topk=min(MOBA_TOPK, nb),
                             scale=HEAD_DIM ** -0.5)
    return pl.pallas_call(
        kern,
        out_shape=jax.ShapeDtypeStruct((b, s, n_heads * HEAD_DIM), BF16),
        grid_spec=pltpu.PrefetchScalarGridSpec(
            num_scalar_prefetch=1,
            grid=(b, n_heads, nb),
            in_specs=[
                pl.BlockSpec((None, blk, HEAD_DIM), lambda bi, h, qi, sl: (bi, qi, q_col + h)),
                pl.BlockSpec((None, s, HEAD_DIM), lambda bi, h, qi, sl: (bi, 0, k_col + h)),
                pl.BlockSpec((None, s, HEAD_DIM), lambda bi, h, qi, sl: (bi, 0, v_col + h)),
            ],
            out_specs=pl.BlockSpec((None, blk, HEAD_DIM), lambda bi, h, qi, sl: (bi, qi, h)),
            scratch_shapes=[
                pltpu.VMEM((LANES, HEAD_DIM), F32),
                pltpu.VMEM((blk, 1), F32),
                pltpu.VMEM((blk, 1), F32),
                pltpu.VMEM((blk, HEAD_DIM), F32),
            ],
        ),
        compiler_params=pltpu.CompilerParams(
            dimension_semantics=("parallel", "parallel", "arbitrary"),
            vmem_limit_bytes=32 * 1024 * 1024),
        name="moba",
    )(slopes, proj, proj, proj)


def _diff_kernel(slopes_ref, lam_ref, q0_ref, q1_ref, k0_ref, k1_ref, v_ref, gain_ref, o_ref,
                 m_ref, l_ref, acc_ref, *, tq, scale, out_scale):
    h = pl.program_id(1)
    qi = pl.program_id(2)
    slope = slopes_ref[h]
    lam = lam_ref[0]
    row = lax.broadcasted_iota(jnp.int32, (tq, tq), 0)
    col = lax.broadcasted_iota(jnp.int32, (tq, tq), 1)
    nslope_rel = (row - col).astype(F32) * (-slope)
    own = pl.multiple_of(qi * tq, tq)

    def attend(q_ref, k_ref):
        q = q_ref[...]
        s = _nt_dot(q, k_ref[pl.ds(own, tq), :]) * scale + nslope_rel
        s = jnp.where(col <= row, s, NEG_INF)
        m0 = jnp.max(s, axis=-1, keepdims=True)
        p = jnp.exp(s - m0)
        m_ref[...] = m0
        l_ref[...] = jnp.sum(p, axis=-1, keepdims=True)
        acc_ref[...] = _dot(p.astype(BF16), v_ref[pl.ds(own, tq), :])

        def past_block(n, carry):
            start = pl.multiple_of(n * tq, tq)
            dist = ((qi - n) * tq).astype(F32)
            s = _nt_dot(q, k_ref[pl.ds(start, tq), :]) * scale + (nslope_rel - slope * dist)
            m_old = m_ref[...]
            m_new = jnp.maximum(m_old, jnp.max(s, axis=-1, keepdims=True))
            alpha = jnp.exp(m_old - m_new)
            p = jnp.exp(s - m_new)
            m_ref[...] = m_new
            l_ref[...] = alpha * l_ref[...] + jnp.sum(p, axis=-1, keepdims=True)
            acc_ref[...] = alpha * acc_ref[...] + _dot(p.astype(BF16), v_ref[pl.ds(start, tq), :])
            return carry

        lax.fori_loop(0, qi, past_block, 0)
        return acc_ref[...] / l_ref[...]

    o = attend(q0_ref, k0_ref)
    o = o - lam * attend(q1_ref, k1_ref)
    ms = jnp.mean(o * o, axis=-1, keepdims=True)
    o_ref[...] = (o * lax.rsqrt(ms + RMS_EPS) * gain_ref[...] * out_scale).astype(o_ref.dtype)


def _diff_attention(proj, slopes, lam, head_gain, *, n_heads, q_col, k_col, v_col2, out_scale):
    b, s, _ = proj.shape
    tq = ATT_BLOCK
    dv = 2 * HEAD_DIM
    kern = functools.partial(_diff_kernel, tq=tq, scale=HEAD_DIM ** -0.5, out_scale=out_scale)
    qspec = lambda m: pl.BlockSpec((None, tq, HEAD_DIM),
                                   lambda bi, h, qi, sl, lm: (bi, qi, q_col + 2 * h + m))
    kspec = lambda m: pl.BlockSpec((None, s, HEAD_DIM),
                                   lambda bi, h, qi, sl, lm: (bi, 0, k_col + 2 * h + m))
    return pl.pallas_call(
        kern,
        out_shape=jax.ShapeDtypeStruct((b, s, n_heads * dv), BF16),
        grid_spec=pltpu.PrefetchScalarGridSpec(
            num_scalar_prefetch=2,
            grid=(b, n_heads, s // tq),
            in_specs=[
                qspec(0), qspec(1), kspec(0), kspec(1),
                pl.BlockSpec((None, s, dv), lambda bi, h, qi, sl, lm: (bi, 0, v_col2 + h)),
                pl.BlockSpec((1, dv), lambda bi, h, qi, sl, lm: (0, 0)),
            ],
            out_specs=pl.BlockSpec((None, tq, dv), lambda bi, h, qi, sl, lm: (bi, qi, h)),
            scratch_shapes=[
                pltpu.VMEM((tq, 1), F32),
                pltpu.VMEM((tq, 1), F32),
                pltpu.VMEM((tq, dv), F32),
            ],
        ),
        compiler_params=pltpu.CompilerParams(
            dimension_semantics=("parallel", "parallel", "arbitrary"),
            vmem_limit_bytes=32 * 1024 * 1024),
        name="diff_attn",
    )(slopes, lam, proj, proj, proj, proj, proj, head_gain)


def _sb_kernel(q_ref, k_ref, v_ref, o_ref, c_ref, acc_ref, *, tq, scale):
    qi = pl.program_id(2)
    q = q_ref[...]
    row = lax.broadcasted_iota(jnp.int32, (tq, tq), 0)
    col = lax.broadcasted_iota(jnp.int32, (tq, tq), 1)
    strict = col < row
    upper = jnp.where(row > col, 1.0, 0.0).astype(BF16)

    def key_block(n, diagonal):
        start = pl.multiple_of(n * tq, tq)
        z = _nt_dot(q, k_ref[pl.ds(start, tq), :]) * scale
        softplus = jnp.maximum(z, 0.0) + jnp.log1p(jnp.exp(-jnp.abs(z)))
        log_keep = -softplus
        if diagonal:
            log_keep = jnp.where(strict, log_keep, 0.0)
        lk_hi, lk_lo = _split_bf16(log_keep)
        after = _dot(lk_hi, upper) + _dot(lk_lo, upper) + c_ref[...]
        a = jnp.exp(z - softplus + after)
        if diagonal:
            a = jnp.where(strict, a, 0.0)
        acc_ref[...] += _dot(a.astype(BF16), v_ref[pl.ds(start, tq), :])
        c_ref[...] += jnp.sum(log_keep, axis=-1, keepdims=True)

    c_ref[...] = jnp.zeros_like(c_ref)
    acc_ref[...] = jnp.zeros_like(acc_ref)
    key_block(qi, True)

    def past_block(i, carry):
        key_block(qi - 1 - i, False)
        return carry

    lax.fori_loop(0, qi, past_block, 0)
    o_ref[...] = acc_ref[...].astype(o_ref.dtype)


def _sb_attention(proj, *, n_heads, q_col, k_col, v_col):
    b, s, _ = proj.shape
    tq = ATT_BLOCK
    kern = functools.partial(_sb_kernel, tq=tq, scale=HEAD_DIM ** -0.5)
    return pl.pallas_call(
        kern,
        out_shape=jax.ShapeDtypeStruct((b, s, n_heads * HEAD_DIM), BF16),
        grid=(b, n_heads, s // tq),
        in_specs=[
            pl.BlockSpec((None, tq, HEAD_DIM), lambda bi, h, qi: (bi, qi, q_col + h)),
            pl.BlockSpec((None, s, HEAD_DIM), lambda bi, h, qi: (bi, 0, k_col + h)),
            pl.BlockSpec((None, s, HEAD_DIM), lambda bi, h, qi: (bi, 0, v_col + h)),
        ],
        out_specs=pl.BlockSpec((None, tq, HEAD_DIM), lambda bi, h, qi: (bi, qi, h)),
        scratch_shapes=[
            pltpu.VMEM((tq, 1), F32),
            pltpu.VMEM((tq, HEAD_DIM), F32),
        ],
        compiler_params=pltpu.CompilerParams(
            dimension_semantics=("parallel", "parallel", "arbitrary"),
            vmem_limit_bytes=32 * 1024 * 1024),
        name="stick_breaking",
    )(proj, proj, proj)


def _merge_kernel(oa_ref, ob_ref, oc_ref, ga_ref, gb_ref, gc_ref, x_ref, wb_ref, wo_ref, fg_ref,
                  wr_hi_ref, wr_lo_ref, xo_ref, h_ref, lg_ref):
    mixed = ga_ref[...].astype(F32) * _dot(oa_ref[...], wb_ref[0])
    mixed += gb_ref[...].astype(F32) * _dot(ob_ref[...], wb_ref[1])
    mixed += gc_ref[...].astype(F32) * _dot(oc_ref[...], wb_ref[2])
    xn = x_ref[...] + _dot(mixed.astype(BF16), wo_ref[...])
    xo_ref[...] = xn
    ms = jnp.mean(xn * xn, axis=-1, keepdims=True)
    hn = xn * lax.rsqrt(ms + RMS_EPS) * fg_ref[...]
    h_hi, h_lo = _split_bf16(hn)
    h_ref[...] = h_hi
    lg_ref[...] = (_dot(h_hi, wr_hi_ref[...]) + _dot(h_hi, wr_lo_ref[...])
                   + _dot(h_lo, wr_hi_ref[...]))


def _merge(oa, ob, oc, proj2d, x2d, wb, wo, fg, wr_hi, wr_lo, *, tm, gate_col):
    t, d = x2d.shape
    mw = oa.shape[1]
    nr = wr_hi.shape[1]
    const = lambda *shape: pl.BlockSpec(shape, lambda i: (0,) * len(shape),
                                        pipeline_mode=pl.Buffered(1))
    gspec = lambda g: pl.BlockSpec((tm, d), lambda i: (i, gate_col + g))
    return pl.pallas_call(
        _merge_kernel,
        out_shape=(jax.ShapeDtypeStruct((t, d), F32),
                   jax.ShapeDtypeStruct((t, d), BF16),
                   jax.ShapeDtypeStruct((t, nr), F32)),
        grid=(t // tm,),
        in_specs=[
            pl.BlockSpec((tm, mw), lambda i: (i, 0)),
            pl.BlockSpec((tm, mw), lambda i: (i, 0)),
            pl.BlockSpec((tm, mw), lambda i: (i, 0)),
            gspec(0), gspec(1), gspec(2),
            pl.BlockSpec((tm, d), lambda i: (i, 0)),
            const(N_BRANCH, mw, d),
            const(d, d),
            const(1, d),
            const(d, nr),
            const(d, nr),
        ],
        out_specs=(pl.BlockSpec((tm, d), lambda i: (i, 0)),
                   pl.BlockSpec((tm, d), lambda i: (i, 0)),
                   pl.BlockSpec((tm, nr), lambda i: (i, 0))),
        compiler_params=pltpu.CompilerParams(
            dimension_semantics=("parallel",),
            vmem_limit_bytes=56 * 1024 * 1024),
        name="merge_out",
    )(oa, ob, oc, proj2d, proj2d, proj2d, x2d, wb, wo, fg, wr_hi, wr_lo)


def _moe_kernel(be_ref, nused_ref, xs_ref, wg_ref, wu_ref, wd_ref, ys_ref):
    i = pl.program_id(0)

    @pl.when(i < nused_ref[0])
    def _():
        x = xs_ref[...]
        g = _dot(x, wg_ref[...])
        u = _dot(x, wu_ref[...])
        act = g * (1.0 / (1.0 + jnp.exp(-g))) * u
        ys_ref[...] = _dot(act.astype(BF16), wd_ref[...]).astype(ys_ref.dtype)

    @pl.when(i >= nused_ref[0])
    def _():
        ys_ref[...] = jnp.zeros_like(ys_ref)


def _moe_experts(xs, block_expert, n_used, wg, wu, wd):
    n_slots, d = xs.shape
    ff = wg.shape[2]
    n_blocks = n_slots // MOE_BLOCK
    return pl.pallas_call(
        _moe_kernel,
        out_shape=jax.ShapeDtypeStruct((n_slots, d), F32),
        grid_spec=pltpu.PrefetchScalarGridSpec(
            num_scalar_prefetch=2,
            grid=(n_blocks,),
            in_specs=[
                pl.BlockSpec((MOE_BLOCK, d), lambda i, be, nu: (i, 0)),
                pl.BlockSpec((None, d, ff), lambda i, be, nu: (be[i], 0, 0)),
                pl.BlockSpec((None, d, ff), lambda i, be, nu: (be[i], 0, 0)),
                pl.BlockSpec((None, ff, d), lambda i, be, nu: (be[i], 0, 0)),
            ],
            out_specs=pl.BlockSpec((MOE_BLOCK, d), lambda i, be, nu: (i, 0)),
        ),
        compiler_params=pltpu.CompilerParams(
            dimension_semantics=("arbitrary",),
            vmem_limit_bytes=40 * 1024 * 1024),
        name="moe_experts",
    )(block_expert, n_used, xs, wg, wu, wd)


def _route(logits):
    t = logits.shape[0]
    g_prob = jax.nn.softmax(logits[:, :N_GROUPS], axis=-1)
    g_top, g_idx = lax.top_k(g_prob, 1)
    e_logits = logits[:, N_GROUPS:N_GROUPS + N_EXPERTS].reshape(t, N_GROUPS, EXPERTS_PER_GROUP)
    e_sel = jnp.take_along_axis(e_logits, g_idx[:, :, None], axis=1)[:, 0]
    e_top, e_local = lax.top_k(jax.nn.softmax(e_sel, axis=-1), EXPERT_TOPK)
    weights = g_top * e_top / jnp.sum(e_top, axis=-1, keepdims=True)
    expert = g_idx * EXPERTS_PER_GROUP + e_local
    return weights, expert


def _dispatch_plan(expert):
    t = expert.shape[0]
    n_assign = t * EXPERT_TOPK
    flat_e = expert.reshape(-1)
    order = jnp.argsort(flat_e)
    se = flat_e[order]
    counts = jnp.bincount(flat_e, length=N_EXPERTS)
    padded = ((counts + MOE_BLOCK - 1) // MOE_BLOCK) * MOE_BLOCK
    pend = jnp.cumsum(padded)
    pstart = pend - padded
    cstart = jnp.cumsum(counts) - counts
    dest = pstart[se] + (jnp.arange(n_assign) - cstart[se])
    n_blocks = -(-n_assign // MOE_BLOCK) + N_EXPERTS
    n_slots = n_blocks * MOE_BLOCK
    block_expert = jnp.minimum(
        jnp.searchsorted(pend, jnp.arange(n_blocks) * MOE_BLOCK, side='right'), N_EXPERTS - 1)
    slot_token = jnp.zeros((n_slots,), jnp.int32).at[dest].set((order // EXPERT_TOPK).astype(jnp.int32))
    slot_valid = jnp.zeros((n_slots,), jnp.bool_).at[dest].set(True)
    assign_slot = jnp.zeros((n_assign,), jnp.int32).at[order].set(dest.astype(jnp.int32))
    n_used = (pend[-1] // MOE_BLOCK).astype(jnp.int32).reshape(1)
    return slot_token, slot_valid, assign_slot.reshape(t, EXPERT_TOPK), block_expert.astype(jnp.int32), n_used


def _alibi_slopes(n_heads):
    return jnp.exp2(-8.0 * jnp.arange(1, n_heads + 1, dtype=F32) / n_heads)


def kernel(x, attn_norm_g, w_in, gate_bias, moba_q_gain, moba_k_gain, diff_q_gain, diff_k_gain,
           diff_lambda_q1, diff_lambda_k1, diff_lambda_q2, diff_lambda_k2, diff_head_gain,
           w_branch, w_out, ffn_norm_g, w_router_group, w_router_expert,
           w_expert_gate, w_expert_up, w_expert_down):
    b, s, d = x.shape
    depth = w_in.shape[0]
    mw = d // 2
    heads = mw // HEAD_DIM
    diff_heads = mw // (2 * HEAD_DIM)
    t = b * s
    tn = min(1024, mw)
    assert mw % tn == 0 and d % tn == 0

    def seg(mixer, part):
        start = (mixer * 3 + part) * mw
        return slice(start, start + mw)
    gate_cols = slice(9 * mw, 9 * mw + N_BRANCH * d)
    col_order = [gate_cols, seg(0, 0), seg(0, 1), seg(1, 0), seg(1, 1),
                 seg(0, 2), seg(1, 2), seg(2, 0), seg(2, 1), seg(2, 2)]
    n_gate_blk = N_BRANCH * d // tn
    per = mw // tn
    kinds = jnp.asarray([KIND_GATE] * n_gate_blk + [KIND_NORM] * (4 * per) + [KIND_PLAIN] * (5 * per),
                        jnp.int32)
    base = N_BRANCH * d // HEAD_DIM
    hb = mw // HEAD_DIM
    aq, ak, bq, bk, av, bv, cq, ck, cv = (base + i * hb for i in range(9))

    slopes_moba = _alibi_slopes(heads)
    slopes_diff = _alibi_slopes(diff_heads)
    x2d = x.reshape(t, d)

    for l in range(depth):
        w_perm = jnp.concatenate([w_in[l][:, c] for c in col_order], axis=1).astype(BF16)
        ones = jnp.ones((mw,), F32)
        zeros = jnp.zeros((mw,), F32)
        gain_cols = jnp.concatenate(
            [jnp.ones((N_BRANCH * d,), F32),
             jnp.tile(moba_q_gain[l], hb), jnp.tile(moba_k_gain[l], hb),
             jnp.tile(diff_q_gain[l], hb), jnp.tile(diff_k_gain[l], hb)] + [ones] * 5)[None, :]
        bias_cols = jnp.concatenate([gate_bias[l].reshape(-1)] + [zeros] * 9)[None, :]

        proj2d = _in_proj(x2d, attn_norm_g[l][None, :], w_perm, gain_cols, bias_cols, kinds,
                          tm=512, tn=tn)
        proj = proj2d.reshape(b, s, -1)

        oa = _moba(proj, slopes_moba, n_heads=heads, q_col=aq, k_col=ak, v_col=av)

        lam_init = 0.8 - 0.6 * math.exp(-0.3 * l)
        lam = (jnp.exp(jnp.sum(diff_lambda_q1[l] * diff_lambda_k1[l]))
               - jnp.exp(jnp.sum(diff_lambda_q2[l] * diff_lambda_k2[l])) + lam_init)
        ob = _diff_attention(proj, slopes_diff, lam.reshape(1).astype(F32), diff_head_gain[l][None, :],
                             n_heads=diff_heads, q_col=bq, k_col=bk, v_col2=bv // 2,
                             out_scale=1.0 - lam_init)

        oc = _sb_attention(proj, n_heads=heads, q_col=cq, k_col=ck, v_col=cv)

        w_router = jnp.concatenate([w_router_group[l], w_router_expert[l]], axis=1)
        w_router = jnp.pad(w_router, ((0, 0), (0, LANES - w_router.shape[1])))
        wr_hi, wr_lo = _split_bf16(w_router)
        x2d, h2d, logits = _merge(
            oa.reshape(t, mw), ob.reshape(t, mw), oc.reshape(t, mw), proj2d, x2d,
            w_branch[l].astype(BF16), w_out[l].astype(BF16), ffn_norm_g[l][None, :],
            wr_hi, wr_lo, tm=256, gate_col=0)

        weights, expert = _route(logits)
        slot_token, slot_valid, assign_slot, block_expert, n_used = _dispatch_plan(expert)
        xs = jnp.where(slot_valid[:, None], h2d[slot_token], jnp.zeros((), BF16))
        ys = _moe_experts(xs, block_expert, n_used, w_expert_gate[l].astype(BF16),
                          w_expert_up[l].astype(BF16), w_expert_down[l].astype(BF16))
        y = (weights[:, 0:1] * ys[assign_slot[:, 0]] + weights[:, 1:2] * ys[assign_slot[:, 1]])
        x2d = x2d + y.astype(x2d.dtype)

    return x2d.reshape(b, s, d)
```

```python
import functools
import math

import jax
import jax.numpy as jnp
from jax import lax
from jax.experimental import pallas as pl
from jax.experimental.pallas import tpu as pltpu

F32 = jnp.float32
BF16 = jnp.bfloat16

HEAD_DIM = 128
N_BRANCH = 3
MOBA_BLOCK = 256
MOBA_TOPK = 3
N_GROUPS = 4
EXPERTS_PER_GROUP = 8
N_EXPERTS = N_GROUPS * EXPERTS_PER_GROUP
EXPERT_TOPK = 2
MOE_BLOCK = 256
RMS_EPS = 1e-6
ATT_BLOCK = 256
LANES = 128
NEG_INF = float("-inf")

KIND_PLAIN, KIND_NORM, KIND_GATE = 0, 1, 2


def _nt_dot(a, b):
    return lax.dot_general(a, b, (((1,), (1,)), ((), ())), preferred_element_type=F32)


def _dot(a, b):
    return jnp.dot(a, b, preferred_element_type=F32)


def _split_bf16(x):
    hi = x.astype(BF16)
    lo = (x - hi.astype(F32)).astype(BF16)
    return hi, lo


def _row_max(x):
    return jnp.max(x, axis=-1, keepdims=True)


def _row_sum(x):
    return jnp.sum(x, axis=-1, keepdims=True)


def _in_proj_kernel(kind_ref, x_ref, g_ref, w_ref, gain_ref, bias_ref, o_ref, xn_ref):
    j = pl.program_id(1)

    @pl.when(j == 0)
    def _():
        x = x_ref[...]
        ms = jnp.mean(x * x, axis=-1, keepdims=True)
        xn_ref[...] = (x * lax.rsqrt(ms + RMS_EPS) * g_ref[...]).astype(BF16)

    acc = _dot(xn_ref[...], w_ref[...])
    kind = kind_ref[j]

    @pl.when(kind == KIND_PLAIN)
    def _():
        o_ref[...] = acc.astype(o_ref.dtype)

    @pl.when(kind == KIND_NORM)
    def _():
        for c in range(acc.shape[1] // HEAD_DIM):
            sl = slice(c * HEAD_DIM, (c + 1) * HEAD_DIM)
            blk = acc[:, sl]
            ms = jnp.mean(blk * blk, axis=-1, keepdims=True)
            o_ref[:, sl] = (blk * lax.rsqrt(ms + RMS_EPS) * gain_ref[:, sl]).astype(o_ref.dtype)

    @pl.when(kind == KIND_GATE)
    def _():
        z = acc + bias_ref[...]
        o_ref[...] = (0.5 * jnp.tanh(0.5 * z) + 0.5).astype(o_ref.dtype)


def _in_proj(x2d, g, w, gain_cols, bias_cols, kinds, *, tm, tn):
    t, d = x2d.shape
    n = w.shape[1]
    grid = (t // tm, n // tn)
    return pl.pallas_call(
        _in_proj_kernel,
        out_shape=jax.ShapeDtypeStruct((t, n), BF16),
        grid_spec=pltpu.PrefetchScalarGridSpec(
            num_scalar_prefetch=1,
            grid=grid,
            in_specs=[
                pl.BlockSpec((tm, d), lambda i, j, k: (i, 0)),
                pl.BlockSpec((1, d), lambda i, j, k: (0, 0)),
                pl.BlockSpec((d, tn), lambda i, j, k: (0, j)),
                pl.BlockSpec((1, tn), lambda i, j, k: (0, j)),
                pl.BlockSpec((1, tn), lambda i, j, k: (0, j)),
            ],
            out_specs=pl.BlockSpec((tm, tn), lambda i, j, k: (i, j)),
            scratch_shapes=[pltpu.VMEM((tm, d), BF16)],
        ),
        compiler_params=pltpu.CompilerParams(
            dimension_semantics=("parallel", "arbitrary"),
            vmem_limit_bytes=48 * 1024 * 1024),
        name="in_proj",
    )(kinds, x2d, g, w, gain_cols, bias_cols)


LOG2E = 1.4426950408889634


def _alibi_table(slopes, tq, s):
    r = jnp.arange(tq, dtype=jnp.int32)[:, None]
    j = jnp.arange(s, dtype=jnp.int32)[None, :]
    dist = (r - j + (s - tq)).astype(F32)
    return jnp.where(dist >= 0.0, (-LOG2E * slopes)[:, None, None] * dist[None], NEG_INF)


def _per_tile(qi, n_tiles, body):
    for c in range(n_tiles):
        pl.when(qi == c)(functools.partial(body, c))


def _moba_kernel(q_ref, k_ref, v_ref, bias_ref, bmap_ref, o_ref, km_ref, *, nb, blk, topk, scale):
    qi = pl.program_id(2)
    s_len = nb * blk

    @pl.when(qi == 0)
    def _():
        km_ref[...] = jnp.zeros_like(km_ref)
        for n in range(nb):
            kblk = k_ref[n * blk:(n + 1) * blk, :].astype(F32)
            km_ref[n:n + 1, :] = jnp.sum(kblk, axis=0, keepdims=True) * (1.0 / blk)

    q = q_ref[...]
    km_hi, km_lo = _split_bf16(km_ref[...])
    gate = _nt_dot(q, km_hi) + _nt_dot(q, km_lo)
    lane = lax.broadcasted_iota(jnp.int32, gate.shape, 1)
    lane_f = lane.astype(F32)
    gate = jnp.where(lane < qi, gate, NEG_INF)
    sel = jnp.where(lane == qi, 1.0, 0.0)
    for _ in range(topk):
        gmax = _row_max(gate)
        first = jnp.min(jnp.where(gate == gmax, lane_f, float(LANES)), axis=-1, keepdims=True)
        pick = jnp.logical_and(lane_f == first, gmax > NEG_INF)
        sel = jnp.where(pick, 1.0, sel)
        gate = jnp.where(pick, NEG_INF, gate)
    sel = sel.astype(BF16)

    def tile(c):
        w = (c + 1) * blk
        s = _nt_dot(q, k_ref[0:w, :]) * scale + bias_ref[:, s_len - w:s_len]
        chosen = _dot(sel, bmap_ref[:, 0:w])
        s = jnp.where(chosen > 0.5, s, NEG_INF)
        p = jnp.exp2(s - _row_max(s))
        o = _dot(p.astype(BF16), v_ref[0:w, :]) / _row_sum(p)
        o_ref[...] = o.astype(o_ref.dtype)

    _per_tile(qi, nb, tile)


def _moba(proj, slopes, *, n_heads, q_col, k_col, v_col):
    b, s, _ = proj.shape
    blk = MOBA_BLOCK
    nb = s // blk
    assert s % blk == 0 and nb <= LANES
    bias = _alibi_table(slopes, blk, s)
    bmap = (jnp.arange(LANES, dtype=jnp.int32)[:, None]
            == (jnp.arange(s, dtype=jnp.int32) // blk)[None, :]).astype(BF16)
    kern = functools.partial(_moba_kernel, nb=nb, blk=blk, topk=min(MOBA_TOPK, nb),
                             scale=LOG2E * HEAD_DIM ** -0.5)
    return pl.pallas_call(
        kern,
        out_shape=jax.ShapeDtypeStruct((b, s, n_heads * HEAD_DIM), BF16),
        grid=(b, n_heads, nb),
        in_specs=[
            pl.BlockSpec((None, blk, HEAD_DIM), lambda bi, h, qi: (bi, qi, q_col + h)),
            pl.BlockSpec((None, s, HEAD_DIM), lambda bi, h, qi: (bi, 0, k_col + h)),
            pl.BlockSpec((None, s, HEAD_DIM), lambda bi, h, qi: (bi, 0, v_col + h)),
            pl.BlockSpec((None, blk, s), lambda bi, h, qi: (h, 0, 0)),
            pl.BlockSpec((LANES, s), lambda bi, h, qi: (0, 0)),
        ],
        out_specs=pl.BlockSpec((None, blk, HEAD_DIM), lambda bi, h, qi: (bi, qi, h)),
        scratch_shapes=[pltpu.VMEM((LANES, HEAD_DIM), F32)],
        compiler_params=pltpu.CompilerParams(
            dimension_semantics=("parallel", "parallel", "arbitrary"),
            vmem_limit_bytes=48 * 1024 * 1024),
        name="moba",
    )(proj, proj, proj, bias, bmap)


def _diff_kernel(lam_ref, q0_ref, q1_ref, k0_ref, k1_ref, v_ref, bias_ref, gain_ref, o_ref,
                 *, nq, tq, scale, out_scale):
    qi = pl.program_id(2)
    lam = lam_ref[0]
    s_len = nq * tq

    def tile(c):
        w = (c + 1) * tq
        bias = bias_ref[:, s_len - w:s_len]
        v = v_ref[0:w, :]

        def attend(q_ref, k_ref):
            s = _nt_dot(q_ref[...], k_ref[0:w, :]) * scale + bias
            p = jnp.exp2(s - _row_max(s))
            return _dot(p.astype(BF16), v) / _row_sum(p)

        o = attend(q0_ref, k0_ref) - lam * attend(q1_ref, k1_ref)
        ms = jnp.mean(o * o, axis=-1, keepdims=True)
        o_ref[...] = (o * lax.rsqrt(ms + RMS_EPS) * gain_ref[...] * out_scale).astype(o_ref.dtype)

    _per_tile(qi, nq, tile)


def _diff_attention(proj, slopes, lam, head_gain, *, n_heads, q_col, k_col, v_col2, out_scale):
    b, s, _ = proj.shape
    tq = ATT_BLOCK
    nq = s // tq
    dv = 2 * HEAD_DIM
    bias = _alibi_table(slopes, tq, s)
    kern = functools.partial(_diff_kernel, nq=nq, tq=tq, scale=LOG2E * HEAD_DIM ** -0.5,
                             out_scale=out_scale)
    qspec = lambda m: pl.BlockSpec((None, tq, HEAD_DIM),
                                   lambda bi, h, qi, lm: (bi, qi, q_col + 2 * h + m))
    kspec = lambda m: pl.BlockSpec((None, s, HEAD_DIM),
                                   lambda bi, h, qi, lm: (bi, 0, k_col + 2 * h + m))
    return pl.pallas_call(
        kern,
        out_shape=jax.ShapeDtypeStruct((b, s, n_heads * dv), BF16),
        grid_spec=pltpu.PrefetchScalarGridSpec(
            num_scalar_prefetch=1,
            grid=(b, n_heads, nq),
            in_specs=[
                qspec(0), qspec(1), kspec(0), kspec(1),
                pl.BlockSpec((None, s, dv), lambda bi, h, qi, lm: (bi, 0, v_col2 + h)),
                pl.BlockSpec((None, tq, s), lambda bi, h, qi, lm: (h, 0, 0)),
                pl.BlockSpec((1, dv), lambda bi, h, qi, lm: (0, 0)),
            ],
            out_specs=pl.BlockSpec((None, tq, dv), lambda bi, h, qi, lm: (bi, qi, h)),
        ),
        compiler_params=pltpu.CompilerParams(
            dimension_semantics=("parallel", "parallel", "arbitrary"),
            vmem_limit_bytes=48 * 1024 * 1024),
        name="diff_attn",
    )(lam, proj, proj, proj, proj, proj, bias, head_gain)


def _sb_kernel(q_ref, k_ref, v_ref, o_ref, *, nq, tq, scale):
    qi = pl.program_id(2)
    q = q_ref[...]
    row = lax.broadcasted_iota(jnp.int32, (tq, tq), 0)
    col = lax.broadcasted_iota(jnp.int32, (tq, tq), 1)
    strict = col < row
    upper = jnp.where(row > col, 1.0, 0.0).astype(BF16)

    def tile(c):
        carry = None
        out = None
        for n in range(c, -1, -1):
            ks = slice(n * tq, (n + 1) * tq)
            z = _nt_dot(q, k_ref[ks, :]) * scale
            softplus = jnp.maximum(z, 0.0) + jnp.log(1.0 + jnp.exp(-jnp.abs(z)))
            log_keep = -softplus
            if n == c:
                log_keep = jnp.where(strict, log_keep, 0.0)
            lk_hi, lk_lo = _split_bf16(log_keep)
            after = _dot(lk_hi, upper) + _dot(lk_lo, upper)
            if carry is not None:
                after = after + carry
            a = jnp.exp(z - softplus + after)
            if n == c:
                a = jnp.where(strict, a, 0.0)
            pv = _dot(a.astype(BF16), v_ref[ks, :])
            out = pv if out is None else out + pv
            if n > 0:
                rs = _row_sum(log_keep)
                carry = rs if carry is None else carry + rs
        o_ref[...] = out.astype(o_ref.dtype)

    _per_tile(qi, nq, tile)


def _sb_attention(proj, *, n_heads, q_col, k_col, v_col):
    b, s, _ = proj.shape
    tq = ATT_BLOCK
    nq = s // tq
    kern = functools.partial(_sb_kernel, nq=nq, tq=tq, scale=HEAD_DIM ** -0.5)
    return pl.pallas_call(
        kern,
        out_shape=jax.ShapeDtypeStruct((b, s, n_heads * HEAD_DIM), BF16),
        grid=(b, n_heads, nq),
        in_specs=[
            pl.BlockSpec((None, tq, HEAD_DIM), lambda bi, h, qi: (bi, qi, q_col + h)),
            pl.BlockSpec((None, s, HEAD_DIM), lambda bi, h, qi: (bi, 0, k_col + h)),
            pl.BlockSpec((None, s, HEAD_DIM), lambda bi, h, qi: (bi, 0, v_col + h)),
        ],
        out_specs=pl.BlockSpec((None, tq, HEAD_DIM), lambda bi, h, qi: (bi, qi, h)),
        compiler_params=pltpu.CompilerParams(
            dimension_semantics=("parallel", "parallel", "arbitrary"),
            vmem_limit_bytes=48 * 1024 * 1024),
        name="stick_breaking",
    )(proj, proj, proj)


def _route_tile(lg):
    lane = lax.broadcasted_iota(jnp.int32, lg.shape, 1).astype(F32)
    first_of = lambda mask: jnp.min(jnp.where(mask, lane, float(LANES)), axis=-1, keepdims=True)

    is_group = lane < float(N_GROUPS)
    gl = jnp.where(is_group, lg, NEG_INF)
    ge = jnp.where(is_group, jnp.exp(gl - _row_max(gl)), 0.0)
    g_prob = ge / _row_sum(ge)
    g_top = _row_max(g_prob)
    g_idx = first_of(g_prob == g_top)

    e_lo = float(N_GROUPS) + float(EXPERTS_PER_GROUP) * g_idx
    in_group = jnp.logical_and(lane >= e_lo, lane < e_lo + float(EXPERTS_PER_GROUP))
    el = jnp.where(in_group, lg, NEG_INF)
    ee = jnp.where(in_group, jnp.exp(el - _row_max(el)), 0.0)
    e_prob = jnp.where(in_group, ee / _row_sum(ee), -1.0)
    top1 = _row_max(e_prob)
    idx1 = first_of(e_prob == top1)
    e_rest = jnp.where(lane == idx1, -1.0, e_prob)
    top2 = _row_max(e_rest)
    idx2 = first_of(e_rest == top2)
    denom = top1 + top2
    w1 = g_top * top1 / denom
    w2 = g_top * top2 / denom
    out = jnp.where(lane == 0.0, w1, 0.0)
    out = jnp.where(lane == 1.0, w2, out)
    out = jnp.where(lane == 2.0, idx1 - float(N_GROUPS), out)
    out = jnp.where(lane == 3.0, idx2 - float(N_GROUPS), out)
    return out


def _merge_kernel(oa_ref, ob_ref, oc_ref, ga_ref, gb_ref, gc_ref, x_ref, wb_ref, wo_ref, fg_ref,
                  wr_hi_ref, wr_lo_ref, xo_ref, h_ref, route_ref):
    mixed = ga_ref[...].astype(F32) * _dot(oa_ref[...], wb_ref[0])
    mixed += gb_ref[...].astype(F32) * _dot(ob_ref[...], wb_ref[1])
    mixed += gc_ref[...].astype(F32) * _dot(oc_ref[...], wb_ref[2])
    xn = x_ref[...] + _dot(mixed.astype(BF16), wo_ref[...])
    xo_ref[...] = xn
    ms = jnp.mean(xn * xn, axis=-1, keepdims=True)
    hn = xn * lax.rsqrt(ms + RMS_EPS) * fg_ref[...]
    h_hi, h_lo = _split_bf16(hn)
    h_ref[...] = h_hi
    logits = (_dot(h_hi, wr_hi_ref[...]) + _dot(h_hi, wr_lo_ref[...]) + _dot(h_lo, wr_hi_ref[...]))
    route_ref[...] = _route_tile(logits)


def _merge(oa, ob, oc, proj2d, x2d, wb, wo, fg, wr_hi, wr_lo, *, tm, gate_col):
    t, d = x2d.shape
    mw = oa.shape[1]
    nr = wr_hi.shape[1]
    const = lambda *shape: pl.BlockSpec(shape, lambda i: (0,) * len(shape),
                                        pipeline_mode=pl.Buffered(1))
    gspec = lambda g: pl.BlockSpec((tm, d), lambda i: (i, gate_col + g))
    return pl.pallas_call(
        _merge_kernel,
        out_shape=(jax.ShapeDtypeStruct((t, d), F32),
                   jax.ShapeDtypeStruct((t, d), BF16),
                   jax.ShapeDtypeStruct((t, nr), F32)),
        grid=(t // tm,),
        in_specs=[
            pl.BlockSpec((tm, mw), lambda i: (i, 0)),
            pl.BlockSpec((tm, mw), lambda i: (i, 0)),
            pl.BlockSpec((tm, mw), lambda i: (i, 0)),
            gspec(0), gspec(1), gspec(2),
            pl.BlockSpec((tm, d), lambda i: (i, 0)),
            const(N_BRANCH, mw, d),
            const(d, d),
            const(1, d),
            const(d, nr),
            const(d, nr),
        ],
        out_specs=(pl.BlockSpec((tm, d), lambda i: (i, 0)),
                   pl.BlockSpec((tm, d), lambda i: (i, 0)),
                   pl.BlockSpec((tm, nr), lambda i: (i, 0))),
        compiler_params=pltpu.CompilerParams(
            dimension_semantics=("parallel",),
            vmem_limit_bytes=56 * 1024 * 1024),
        name="merge_out",
    )(oa, ob, oc, proj2d, proj2d, proj2d, x2d, wb, wo, fg, wr_hi, wr_lo)


def _moe_kernel(be_ref, nused_ref, xs_ref, wg_ref, wu_ref, wd_ref, ys_ref):
    i = pl.program_id(0)

    @pl.when(i < nused_ref[0])
    def _():
        x = xs_ref[...]
        g = _dot(x, wg_ref[...])
        u = _dot(x, wu_ref[...])
        act = g * (1.0 / (1.0 + jnp.exp(-g))) * u
        ys_ref[...] = _dot(act.astype(BF16), wd_ref[...]).astype(ys_ref.dtype)

    @pl.when(i >= nused_ref[0])
    def _():
        ys_ref[...] = jnp.zeros_like(ys_ref)


def _moe_experts(xs, block_expert, n_used, wg, wu, wd):
    n_slots, d = xs.shape
    ff = wg.shape[2]
    n_blocks = n_slots // MOE_BLOCK
    return pl.pallas_call(
        _moe_kernel,
        out_shape=jax.ShapeDtypeStruct((n_slots, d), F32),
        grid_spec=pltpu.PrefetchScalarGridSpec(
            num_scalar_prefetch=2,
            grid=(n_blocks,),
            in_specs=[
                pl.BlockSpec((MOE_BLOCK, d), lambda i, be, nu: (i, 0)),
                pl.BlockSpec((None, d, ff), lambda i, be, nu: (be[i], 0, 0)),
                pl.BlockSpec((None, d, ff), lambda i, be, nu: (be[i], 0, 0)),
                pl.BlockSpec((None, ff, d), lambda i, be, nu: (be[i], 0, 0)),
            ],
            out_specs=pl.BlockSpec((MOE_BLOCK, d), lambda i, be, nu: (i, 0)),
        ),
        compiler_params=pltpu.CompilerParams(
            dimension_semantics=("arbitrary",),
            vmem_limit_bytes=40 * 1024 * 1024),
        name="moe_experts",
    )(block_expert, n_used, xs, wg, wu, wd)


def _dispatch_plan(expert):
    t = expert.shape[0]
    n_assign = t * EXPERT_TOPK
    flat_e = expert.reshape(-1)
    iota = jnp.arange(n_assign, dtype=jnp.int32)
    _, order = lax.sort((flat_e, iota), num_keys=1, is_stable=True)
    _, rank = lax.sort((order, iota), num_keys=1, is_stable=True)
    ids = jnp.arange(N_EXPERTS, dtype=jnp.int32)
    onehot = flat_e[:, None] == ids[None, :]
    counts = jnp.sum(onehot.astype(jnp.int32), axis=0)
    padded = ((counts + MOE_BLOCK - 1) // MOE_BLOCK) * MOE_BLOCK
    pend = jnp.cumsum(padded)
    pstart = pend - padded
    cstart = jnp.cumsum(counts) - counts
    n_blocks = -(-n_assign // MOE_BLOCK) + N_EXPERTS
    block_start = jnp.arange(n_blocks, dtype=jnp.int32) * MOE_BLOCK
    block_expert = jnp.minimum(jnp.sum((pend[None, :] <= block_start[:, None]).astype(jnp.int32), axis=1),
                               N_EXPERTS - 1)
    off = (block_start - pstart[block_expert])[:, None] + jnp.arange(MOE_BLOCK, dtype=jnp.int32)[None, :]
    slot_valid = (off < counts[block_expert][:, None]).reshape(-1)
    pos = jnp.clip(cstart[block_expert][:, None] + off, 0, n_assign - 1).reshape(-1)
    slot_token = order[pos] // EXPERT_TOPK
    shift = jnp.sum(jnp.where(onehot, (pstart - cstart)[None, :], 0), axis=1)
    assign_slot = (rank + shift).reshape(t, EXPERT_TOPK)
    n_used = (pend[-1] // MOE_BLOCK).astype(jnp.int32).reshape(1)
    return slot_token, slot_valid, assign_slot, block_expert.astype(jnp.int32), n_used


def _alibi_slopes(n_heads):
    return jnp.exp2(-8.0 * jnp.arange(1, n_heads + 1, dtype=F32) / n_heads)


def kernel(x, attn_norm_g, w_in, gate_bias, moba_q_gain, moba_k_gain, diff_q_gain, diff_k_gain,
           diff_lambda_q1, diff_lambda_k1, diff_lambda_q2, diff_lambda_k2, diff_head_gain,
           w_branch, w_out, ffn_norm_g, w_router_group, w_router_expert,
           w_expert_gate, w_expert_up, w_expert_down):
    b, s, d = x.shape
    depth = w_in.shape[0]
    mw = d // 2
    heads = mw // HEAD_DIM
    diff_heads = mw // (2 * HEAD_DIM)
    t = b * s
    tn = min(1024, mw)
    assert mw % tn == 0 and d % tn == 0

    def seg(mixer, part):
        start = (mixer * 3 + part) * mw
        return slice(start, start + mw)
    gate_cols = slice(9 * mw, 9 * mw + N_BRANCH * d)
    col_order = [gate_cols, seg(0, 0), seg(0, 1), seg(1, 0), seg(1, 1),
                 seg(0, 2), seg(1, 2), seg(2, 0), seg(2, 1), seg(2, 2)]
    n_gate_blk = N_BRANCH * d // tn
    per = mw // tn
    kinds = jnp.asarray([KIND_GATE] * n_gate_blk + [KIND_NORM] * (4 * per) + [KIND_PLAIN] * (5 * per),
                        jnp.int32)
    base = N_BRANCH * d // HEAD_DIM
    hb = mw // HEAD_DIM
    aq, ak, bq, bk, av, bv, cq, ck, cv = (base + i * hb for i in range(9))

    slopes_moba = _alibi_slopes(heads)
    slopes_diff = _alibi_slopes(diff_heads)
    x2d = x.reshape(t, d)

    for l in range(depth):
        w_perm = jnp.concatenate([w_in[l][:, c] for c in col_order], axis=1).astype(BF16)
        ones = jnp.ones((mw,), F32)
        zeros = jnp.zeros((mw,), F32)
        gain_cols = jnp.concatenate(
            [jnp.ones((N_BRANCH * d,), F32),
             jnp.tile(moba_q_gain[l], hb), jnp.tile(moba_k_gain[l], hb),
             jnp.tile(diff_q_gain[l], hb), jnp.tile(diff_k_gain[l], hb)] + [ones] * 5)[None, :]
        bias_cols = jnp.concatenate([gate_bias[l].reshape(-1)] + [zeros] * 9)[None, :]

        proj2d = _in_proj(x2d, attn_norm_g[l][None, :], w_perm, gain_cols, bias_cols, kinds,
                          tm=512, tn=tn)
        proj = proj2d.reshape(b, s, -1)

        oa = _moba(proj, slopes_moba, n_heads=heads, q_col=aq, k_col=ak, v_col=av)

        lam_init = 0.8 - 0.6 * math.exp(-0.3 * l)
        lam = (jnp.exp(jnp.sum(diff_lambda_q1[l] * diff_lambda_k1[l]))
               - jnp.exp(jnp.sum(diff_lambda_q2[l] * diff_lambda_k2[l])) + lam_init)
        ob = _diff_attention(proj, slopes_diff, lam.reshape(1).astype(F32), diff_head_gain[l][None, :],
                             n_heads=diff_heads, q_col=bq, k_col=bk, v_col2=bv // 2,
                             out_scale=1.0 - lam_init)

        oc = _sb_attention(proj, n_heads=heads, q_col=cq, k_col=ck, v_col=cv)

        w_router = jnp.concatenate([w_router_group[l], w_router_expert[l]], axis=1)
        w_router = jnp.pad(w_router, ((0, 0), (0, LANES - w_router.shape[1])))
        wr_hi, wr_lo = _split_bf16(w_router)
        x2d, h2d, route = _merge(
            oa.reshape(t, mw), ob.reshape(t, mw), oc.reshape(t, mw), proj2d, x2d,
            w_branch[l].astype(BF16), w_out[l].astype(BF16), ffn_norm_g[l][None, :],
            wr_hi, wr_lo, tm=256, gate_col=0)

        weights = route[:, 0:EXPERT_TOPK]
        expert = route[:, EXPERT_TOPK:2 * EXPERT_TOPK].astype(jnp.int32)
        slot_token, slot_valid, assign_slot, block_expert, n_used = _dispatch_plan(expert)
        xs = jnp.where(slot_valid[:, None], h2d[slot_token], jnp.zeros((), BF16))
        ys = _moe_experts(xs, block_expert, n_used, w_expert_gate[l].astype(BF16),
                          w_expert_up[l].astype(BF16), w_expert_down[l].astype(BF16))
        y = (weights[:, 0:1] * ys[assign_slot[:, 0]] + weights[:, 1:2] * ys[assign_slot[:, 1]])
        x2d = x2d + y.astype(x2d.dtype)

    return x2d.reshape(b, s, d)
```

```python
import functools
import math

import jax
import jax.numpy as jnp
from jax import lax
from jax.experimental import pallas as pl
from jax.experimental.pallas import tpu as pltpu

F32 = jnp.float32
BF16 = jnp.bfloat16

HEAD_DIM = 128
N_BRANCH = 3
MOBA_BLOCK = 256
MOBA_TOPK = 3
N_GROUPS = 4
EXPERTS_PER_GROUP = 8
N_EXPERTS = N_GROUPS * EXPERTS_PER_GROUP
EXPERT_TOPK = 2
MOE_BLOCK = 256
RMS_EPS = 1e-6
ATT_BLOCK = 256
LANES = 128
MXU_DIM = 256
NEG_INF = float("-inf")
LOG2E = 1.4426950408889634
MIB = 1024 * 1024

KIND_PLAIN, KIND_NORM, KIND_GATE = 0, 1, 2


def _nt_dot(a, b):
    return lax.dot_general(a, b, (((1,), (1,)), ((), ())), preferred_element_type=F32)


def _dot(a, b):
    return jnp.dot(a, b, preferred_element_type=F32)


def _split_bf16(x):
    hi = x.astype(BF16)
    lo = (x - hi.astype(F32)).astype(BF16)
    return hi, lo


def _row_max(x):
    return jnp.max(x, axis=-1, keepdims=True)


def _row_sum(x):
    return jnp.sum(x, axis=-1, keepdims=True)


def _in_proj_kernel(x_ref, g_ref, w_ref, vec_ref, o_ref, xn_ref, *, kind, chunk):
    @pl.when(pl.program_id(1) == 0)
    def _():
        x = x_ref[...]
        ms = jnp.mean(x * x, axis=-1, keepdims=True)
        xn_ref[...] = (x * lax.rsqrt(ms + RMS_EPS) * g_ref[...]).astype(BF16)

    xn = xn_ref[...]
    for c in range(o_ref.shape[1] // chunk):
        cols = slice(c * chunk, (c + 1) * chunk)
        acc = _dot(xn, w_ref[:, cols])
        if kind == KIND_PLAIN:
            o_ref[:, cols] = acc.astype(o_ref.dtype)
        elif kind == KIND_GATE:
            z = acc + vec_ref[:, cols]
            o_ref[:, cols] = (0.5 * jnp.tanh(0.5 * z) + 0.5).astype(o_ref.dtype)
        else:
            for hd in range(chunk // HEAD_DIM):
                sub = slice(hd * HEAD_DIM, (hd + 1) * HEAD_DIM)
                hcols = slice(c * chunk + hd * HEAD_DIM, c * chunk + (hd + 1) * HEAD_DIM)
                blk = acc[:, sub]
                ms = jnp.mean(blk * blk, axis=-1, keepdims=True)
                o_ref[:, hcols] = (blk * lax.rsqrt(ms + RMS_EPS) * vec_ref[:, hcols]).astype(o_ref.dtype)


def _in_proj(x2d, g, w, vec, *, kind, tm, tn, name):
    t, d = x2d.shape
    n = w.shape[1]
    kern = functools.partial(_in_proj_kernel, kind=kind, chunk=min(MXU_DIM, tn))
    return pl.pallas_call(
        kern,
        out_shape=jax.ShapeDtypeStruct((t, n), BF16),
        grid=(t // tm, n // tn),
        in_specs=[
            pl.BlockSpec((tm, d), lambda i, j: (i, 0)),
            pl.BlockSpec((1, d), lambda i, j: (0, 0)),
            pl.BlockSpec((d, tn), lambda i, j: (0, j)),
            pl.BlockSpec((1, tn), lambda i, j: (0, j)),
        ],
        out_specs=pl.BlockSpec((tm, tn), lambda i, j: (i, j)),
        scratch_shapes=[pltpu.VMEM((tm, d), BF16)],
        compiler_params=pltpu.CompilerParams(
            dimension_semantics=("parallel", "arbitrary"),
            vmem_limit_bytes=48 * MIB),
        name=name,
    )(x2d, g, w, vec)


def _alibi_table(slopes, tq, s):
    r = jnp.arange(tq, dtype=jnp.int32)[:, None]
    j = jnp.arange(s, dtype=jnp.int32)[None, :]
    dist = (r - j + (s - tq)).astype(F32)
    return jnp.where(dist >= 0.0, (-LOG2E * slopes)[:, None, None] * dist[None], NEG_INF)


def _per_tile(qi, n_tiles, body):
    for c in range(n_tiles):
        pl.when(qi == c)(functools.partial(body, c))


def _moba_kernel(q_ref, qall_ref, k_ref, v_ref, bias_ref, o_ref, sel_ref, *, nb, blk, topk, scale):
    qi = pl.program_id(2)
    s_len = nb * blk

    @pl.when(qi == 0)
    def _():
        row_id = lax.broadcasted_iota(jnp.int32, (LANES, HEAD_DIM), 0)
        kmean = jnp.zeros((LANES, HEAD_DIM), F32)
        for n in range(nb):
            mean_n = jnp.sum(k_ref[n * blk:(n + 1) * blk, :].astype(F32), axis=0, keepdims=True) * (1.0 / blk)
            kmean = jnp.where(row_id == n, mean_n, kmean)
        km_hi, km_lo = _split_bf16(kmean)
        qall = qall_ref[...]
        gate = _nt_dot(qall, km_hi) + _nt_dot(qall, km_lo)
        lane = lax.broadcasted_iota(jnp.int32, gate.shape, 1)
        own = lax.broadcasted_iota(jnp.int32, gate.shape, 0) // blk
        lane_f = lane.astype(F32)
        gate = jnp.where(lane < own, gate, NEG_INF)
        sel = jnp.where(lane == own, 1.0, 0.0)
        for _ in range(topk):
            gmax = _row_max(gate)
            first = jnp.min(jnp.where(gate == gmax, lane_f, float(LANES)), axis=-1, keepdims=True)
            pick = jnp.logical_and(lane_f == first, gmax > NEG_INF)
            sel = jnp.where(pick, 1.0, sel)
            gate = jnp.where(pick, NEG_INF, gate)
        sel_ref[...] = sel

    q = q_ref[...]
    sel = sel_ref[pl.ds(pl.multiple_of(qi * blk, blk), blk), :]

    def tile(c):
        w = (c + 1) * blk
        s = _nt_dot(q, k_ref[0:w, :]) * scale + bias_ref[:, s_len - w:s_len]
        s = jnp.concatenate(
            [jnp.where(sel[:, n:n + 1] > 0.5, s[:, n * blk:(n + 1) * blk], NEG_INF) for n in range(c + 1)],
            axis=1)
        p = jnp.exp2(s - _row_max(s))
        o = _dot(p.astype(BF16), v_ref[0:w, :]) / _row_sum(p)
        o_ref[...] = o.astype(o_ref.dtype)

    _per_tile(qi, nb, tile)


def _moba(qk, rest, slopes, *, n_heads, q_col, k_col, v_col):
    b, s, _ = qk.shape
    blk = MOBA_BLOCK
    nb = s // blk
    assert s % blk == 0 and nb <= LANES
    bias = _alibi_table(slopes, blk, s)
    kern = functools.partial(_moba_kernel, nb=nb, blk=blk, topk=min(MOBA_TOPK, nb),
                             scale=LOG2E * HEAD_DIM ** -0.5)
    return pl.pallas_call(
        kern,
        out_shape=jax.ShapeDtypeStruct((b, s, n_heads * HEAD_DIM), BF16),
        grid=(b, n_heads, nb),
        in_specs=[
            pl.BlockSpec((None, blk, HEAD_DIM), lambda bi, h, qi: (bi, qi, q_col + h)),
            pl.BlockSpec((None, s, HEAD_DIM), lambda bi, h, qi: (bi, 0, q_col + h)),
            pl.BlockSpec((None, s, HEAD_DIM), lambda bi, h, qi: (bi, 0, k_col + h)),
            pl.BlockSpec((None, s, HEAD_DIM), lambda bi, h, qi: (bi, 0, v_col + h)),
            pl.BlockSpec((None, blk, s), lambda bi, h, qi: (h, 0, 0)),
        ],
        out_specs=pl.BlockSpec((None, blk, HEAD_DIM), lambda bi, h, qi: (bi, qi, h)),
        scratch_shapes=[pltpu.VMEM((s, LANES), F32)],
        compiler_params=pltpu.CompilerParams(
            dimension_semantics=("parallel", "parallel", "arbitrary"),
            vmem_limit_bytes=48 * MIB),
        name="moba",
    )(qk, qk, qk, rest, bias)


def _diff_kernel(lam_ref, q0_ref, q1_ref, k0_ref, k1_ref, v_ref, bias_ref, gain_ref, o_ref,
                 *, nq, tq, scale, out_scale):
    qi = pl.program_id(2)
    lam = lam_ref[0]
    s_len = nq * tq

    def tile(c):
        w = (c + 1) * tq
        bias = bias_ref[:, s_len - w:s_len]
        v = v_ref[0:w, :]

        def attend(q_ref, k_ref):
            s = _nt_dot(q_ref[...], k_ref[0:w, :]) * scale + bias
            p = jnp.exp2(s - _row_max(s))
            return _dot(p.astype(BF16), v) / _row_sum(p)

        o = attend(q0_ref, k0_ref) - lam * attend(q1_ref, k1_ref)
        ms = jnp.mean(o * o, axis=-1, keepdims=True)
        o_ref[...] = (o * lax.rsqrt(ms + RMS_EPS) * gain_ref[...] * out_scale).astype(o_ref.dtype)

    _per_tile(qi, nq, tile)


def _diff_attention(qk, rest, slopes, lam, head_gain, *, n_heads, q_col, k_col, v_col2, out_scale):
    b, s, _ = qk.shape
    tq = ATT_BLOCK
    nq = s // tq
    dv = 2 * HEAD_DIM
    bias = _alibi_table(slopes, tq, s)
    kern = functools.partial(_diff_kernel, nq=nq, tq=tq, scale=LOG2E * HEAD_DIM ** -0.5,
                             out_scale=out_scale)
    qspec = lambda m: pl.BlockSpec((None, tq, HEAD_DIM),
                                   lambda bi, h, qi, lm: (bi, qi, q_col + 2 * h + m))
    kspec = lambda m: pl.BlockSpec((None, s, HEAD_DIM),
                                   lambda bi, h, qi, lm: (bi, 0, k_col + 2 * h + m))
    return pl.pallas_call(
        kern,
        out_shape=jax.ShapeDtypeStruct((b, s, n_heads * dv), BF16),
        grid_spec=pltpu.PrefetchScalarGridSpec(
            num_scalar_prefetch=1,
            grid=(b, n_heads, nq),
            in_specs=[
                qspec(0), qspec(1), kspec(0), kspec(1),
                pl.BlockSpec((None, s, dv), lambda bi, h, qi, lm: (bi, 0, v_col2 + h)),
                pl.BlockSpec((None, tq, s), lambda bi, h, qi, lm: (h, 0, 0)),
                pl.BlockSpec((1, dv), lambda bi, h, qi, lm: (0, 0)),
            ],
            out_specs=pl.BlockSpec((None, tq, dv), lambda bi, h, qi, lm: (bi, qi, h)),
        ),
        compiler_params=pltpu.CompilerParams(
            dimension_semantics=("parallel", "parallel", "arbitrary"),
            vmem_limit_bytes=48 * MIB),
        name="diff_attn",
    )(lam, qk, qk, qk, qk, rest, bias, head_gain)


def _sb_kernel(q_ref, k_ref, v_ref, o_ref, *, nq, tq, scale):
    qi = pl.program_id(2)
    q = q_ref[...]
    row = lax.broadcasted_iota(jnp.int32, (tq, tq), 0)
    col = lax.broadcasted_iota(jnp.int32, (tq, tq), 1)
    strict = col < row
    upper = jnp.where(row > col, 1.0, 0.0).astype(BF16)

    def tile(c):
        carry = None
        out = None
        for n in range(c, -1, -1):
            ks = slice(n * tq, (n + 1) * tq)
            nz = _nt_dot(q, k_ref[ks, :]) * (-scale)
            log_keep = jnp.minimum(nz, 0.0) - jnp.log(1.0 + jnp.exp(-jnp.abs(nz)))
            if n == c:
                log_keep = jnp.where(strict, log_keep, 0.0)
            after = _dot(log_keep.astype(BF16), upper)
            if carry is not None:
                after = after + carry
            a = jnp.exp(log_keep - nz + after)
            if n == c:
                a = jnp.where(strict, a, 0.0)
            pv = _dot(a.astype(BF16), v_ref[ks, :])
            out = pv if out is None else out + pv
            if n > 0:
                rs = _row_sum(log_keep)
                carry = rs if carry is None else carry + rs
        o_ref[...] = out.astype(o_ref.dtype)

    _per_tile(qi, nq, tile)


def _sb_attention(rest, *, n_heads, q_col, k_col, v_col):
    b, s, _ = rest.shape
    tq = ATT_BLOCK
    nq = s // tq
    kern = functools.partial(_sb_kernel, nq=nq, tq=tq, scale=HEAD_DIM ** -0.5)
    return pl.pallas_call(
        kern,
        out_shape=jax.ShapeDtypeStruct((b, s, n_heads * HEAD_DIM), BF16),
        grid=(b, n_heads, nq),
        in_specs=[
            pl.BlockSpec((None, tq, HEAD_DIM), lambda bi, h, qi: (bi, qi, q_col + h)),
            pl.BlockSpec((None, s, HEAD_DIM), lambda bi, h, qi: (bi, 0, k_col + h)),
            pl.BlockSpec((None, s, HEAD_DIM), lambda bi, h, qi: (bi, 0, v_col + h)),
        ],
        out_specs=pl.BlockSpec((None, tq, HEAD_DIM), lambda bi, h, qi: (bi, qi, h)),
        compiler_params=pltpu.CompilerParams(
            dimension_semantics=("parallel", "parallel", "arbitrary"),
            vmem_limit_bytes=48 * MIB),
        name="stick_breaking",
    )(rest, rest, rest)


def _route_tile(lg):
    lane = lax.broadcasted_iota(jnp.int32, lg.shape, 1).astype(F32)
    first_of = lambda mask: jnp.min(jnp.where(mask, lane, float(LANES)), axis=-1, keepdims=True)

    is_group = lane < float(N_GROUPS)
    gl = jnp.where(is_group, lg, NEG_INF)
    ge = jnp.where(is_group, jnp.exp(gl - _row_max(gl)), 0.0)
    g_prob = ge / _row_sum(ge)
    g_top = _row_max(g_prob)
    g_idx = first_of(g_prob == g_top)

    e_lo = float(N_GROUPS) + float(EXPERTS_PER_GROUP) * g_idx
    in_group = jnp.logical_and(lane >= e_lo, lane < e_lo + float(EXPERTS_PER_GROUP))
    el = jnp.where(in_group, lg, NEG_INF)
    ee = jnp.where(in_group, jnp.exp(el - _row_max(el)), 0.0)
    e_prob = jnp.where(in_group, ee / _row_sum(ee), -1.0)
    top1 = _row_max(e_prob)
    idx1 = first_of(e_prob == top1)
    e_rest = jnp.where(lane == idx1, -1.0, e_prob)
    top2 = _row_max(e_rest)
    idx2 = first_of(e_rest == top2)
    denom = top1 + top2
    w1 = g_top * top1 / denom
    w2 = g_top * top2 / denom
    out = jnp.where(lane == 0.0, w1, 0.0)
    out = jnp.where(lane == 1.0, w2, out)
    out = jnp.where(lane == 2.0, idx1 - float(N_GROUPS), out)
    out = jnp.where(lane == 3.0, idx2 - float(N_GROUPS), out)
    return out


def _merge_kernel(oa_ref, ob_ref, oc_ref, ga_ref, gb_ref, gc_ref, x_ref, wb_ref, wo_ref, fg_ref,
                  wr_hi_ref, wr_lo_ref, xo_ref, h_ref, route_ref):
    mixed = ga_ref[...].astype(F32) * _dot(oa_ref[...], wb_ref[0])
    mixed += gb_ref[...].astype(F32) * _dot(ob_ref[...], wb_ref[1])
    mixed += gc_ref[...].astype(F32) * _dot(oc_ref[...], wb_ref[2])
    xn = x_ref[...] + _dot(mixed.astype(BF16), wo_ref[...])
    xo_ref[...] = xn
    ms = jnp.mean(xn * xn, axis=-1, keepdims=True)
    hn = xn * lax.rsqrt(ms + RMS_EPS) * fg_ref[...]
    h_hi, h_lo = _split_bf16(hn)
    h_ref[...] = h_hi
    logits = (_dot(h_hi, wr_hi_ref[...]) + _dot(h_hi, wr_lo_ref[...]) + _dot(h_lo, wr_hi_ref[...]))
    route_ref[...] = _route_tile(logits)


def _merge(oa, ob, oc, gates, x2d, wb, wo, fg, wr_hi, wr_lo, *, tm):
    t, d = x2d.shape
    mw = oa.shape[1]
    nr = wr_hi.shape[1]
    const = lambda *shape: pl.BlockSpec(shape, lambda i: (0,) * len(shape),
                                        pipeline_mode=pl.Buffered(1))
    gspec = lambda g: pl.BlockSpec((tm, d), lambda i: (i, g))
    return pl.pallas_call(
        _merge_kernel,
        out_shape=(jax.ShapeDtypeStruct((t, d), F32),
                   jax.ShapeDtypeStruct((t, d), BF16),
                   jax.ShapeDtypeStruct((t, nr), F32)),
        grid=(t // tm,),
        in_specs=[
            pl.BlockSpec((tm, mw), lambda i: (i, 0)),
            pl.BlockSpec((tm, mw), lambda i: (i, 0)),
            pl.BlockSpec((tm, mw), lambda i: (i, 0)),
            gspec(0), gspec(1), gspec(2),
            pl.BlockSpec((tm, d), lambda i: (i, 0)),
            const(N_BRANCH, mw, d),
            const(d, d),
            const(1, d),
            const(d, nr),
            const(d, nr),
        ],
        out_specs=(pl.BlockSpec((tm, d), lambda i: (i, 0)),
                   pl.BlockSpec((tm, d), lambda i: (i, 0)),
                   pl.BlockSpec((tm, nr), lambda i: (i, 0))),
        compiler_params=pltpu.CompilerParams(
            dimension_semantics=("parallel",),
            vmem_limit_bytes=56 * MIB),
        name="merge_out",
    )(oa, ob, oc, gates, gates, gates, x2d, wb, wo, fg, wr_hi, wr_lo)


def _moe_kernel(be_ref, nused_ref, xs_ref, wg_ref, wu_ref, wd_ref, ys_ref, wgb_ref, wub_ref, wdb_ref):
    i = pl.program_id(0)
    live = i < nused_ref[0]
    new_expert = jnp.logical_or(i == 0, be_ref[i] != be_ref[jnp.maximum(i - 1, 0)])

    @pl.when(jnp.logical_and(live, new_expert))
    def _():
        wgb_ref[...] = wg_ref[...].astype(BF16)
        wub_ref[...] = wu_ref[...].astype(BF16)
        wdb_ref[...] = wd_ref[...].astype(BF16)

    @pl.when(live)
    def _():
        x = xs_ref[...]
        g = _dot(x, wgb_ref[...])
        u = _dot(x, wub_ref[...])
        act = g * (1.0 / (1.0 + jnp.exp(-g))) * u
        ys_ref[...] = _dot(act.astype(BF16), wdb_ref[...]).astype(ys_ref.dtype)

    @pl.when(jnp.logical_not(live))
    def _():
        ys_ref[...] = jnp.zeros_like(ys_ref)


def _moe_experts(xs, block_expert, n_used, wg, wu, wd):
    n_slots, d = xs.shape
    ff = wg.shape[2]
    n_blocks = n_slots // MOE_BLOCK
    return pl.pallas_call(
        _moe_kernel,
        out_shape=jax.ShapeDtypeStruct((n_slots, d), F32),
        grid_spec=pltpu.PrefetchScalarGridSpec(
            num_scalar_prefetch=2,
            grid=(n_blocks,),
            in_specs=[
                pl.BlockSpec((MOE_BLOCK, d), lambda i, be, nu: (i, 0)),
                pl.BlockSpec((None, d, ff), lambda i, be, nu: (be[i], 0, 0)),
                pl.BlockSpec((None, d, ff), lambda i, be, nu: (be[i], 0, 0)),
                pl.BlockSpec((None, ff, d), lambda i, be, nu: (be[i], 0, 0)),
            ],
            out_specs=pl.BlockSpec((MOE_BLOCK, d), lambda i, be, nu: (i, 0)),
            scratch_shapes=[pltpu.VMEM((d, ff), BF16), pltpu.VMEM((d, ff), BF16),
                            pltpu.VMEM((ff, d), BF16)],
        ),
        compiler_params=pltpu.CompilerParams(
            dimension_semantics=("arbitrary",),
            vmem_limit_bytes=52 * MIB),
        name="moe_experts",
    )(block_expert, n_used, xs, wg, wu, wd)


def _dispatch_plan(expert):
    t = expert.shape[0]
    n_assign = t * EXPERT_TOPK
    flat_e = expert.reshape(-1)
    iota = jnp.arange(n_assign, dtype=jnp.int32)
    _, order = lax.sort((flat_e, iota), num_keys=1, is_stable=True)
    _, rank = lax.sort((order, iota), num_keys=1, is_stable=True)
    ids = jnp.arange(N_EXPERTS, dtype=jnp.int32)
    onehot = flat_e[:, None] == ids[None, :]
    counts = jnp.sum(onehot.astype(jnp.int32), axis=0)
    padded = ((counts + MOE_BLOCK - 1) // MOE_BLOCK) * MOE_BLOCK
    pend = jnp.cumsum(padded)
    pstart = pend - padded
    cstart = jnp.cumsum(counts) - counts
    n_blocks = -(-n_assign // MOE_BLOCK) + N_EXPERTS
    block_start = jnp.arange(n_blocks, dtype=jnp.int32) * MOE_BLOCK
    block_expert = jnp.minimum(jnp.sum((pend[None, :] <= block_start[:, None]).astype(jnp.int32), axis=1),
                               N_EXPERTS - 1)
    off = (block_start - pstart[block_expert])[:, None] + jnp.arange(MOE_BLOCK, dtype=jnp.int32)[None, :]
    pos = jnp.clip(cstart[block_expert][:, None] + off, 0, n_assign - 1).reshape(-1)
    slot_token = order[pos] // EXPERT_TOPK
    shift = jnp.sum(jnp.where(onehot, (pstart - cstart)[None, :], 0), axis=1)
    assign_slot = (rank + shift).reshape(t, EXPERT_TOPK)
    n_used = (pend[-1] // MOE_BLOCK).astype(jnp.int32).reshape(1)
    return slot_token, assign_slot, block_expert.astype(jnp.int32), n_used


def _alibi_slopes(n_heads):
    return jnp.exp2(-8.0 * jnp.arange(1, n_heads + 1, dtype=F32) / n_heads)


def kernel(x, attn_norm_g, w_in, gate_bias, moba_q_gain, moba_k_gain, diff_q_gain, diff_k_gain,
           diff_lambda_q1, diff_lambda_k1, diff_lambda_q2, diff_lambda_k2, diff_head_gain,
           w_branch, w_out, ffn_norm_g, w_router_group, w_router_expert,
           w_expert_gate, w_expert_up, w_expert_down):
    b, s, d = x.shape
    depth = w_in.shape[0]
    mw = d // 2
    heads = mw // HEAD_DIM
    diff_heads = mw // (2 * HEAD_DIM)
    hb = mw // HEAD_DIM
    t = b * s
    tn = min(1024, mw)
    tm = 512
    assert mw % tn == 0 and d % tn == 0 and t % tm == 0

    def seg(w, mixer, part):
        start = (mixer * 3 + part) * mw
        return w[:, start:start + mw]

    slopes_moba = _alibi_slopes(heads)
    slopes_diff = _alibi_slopes(diff_heads)
    x2d = x.reshape(t, d)

    for l in range(depth):
        wl = w_in[l]
        w_gate = wl[:, 9 * mw:].astype(BF16)
        w_qk = jnp.concatenate([seg(wl, 0, 0), seg(wl, 0, 1), seg(wl, 1, 0), seg(wl, 1, 1)], axis=1).astype(BF16)
        w_rest = jnp.concatenate([seg(wl, 0, 2), seg(wl, 1, 2), seg(wl, 2, 0), seg(wl, 2, 1), seg(wl, 2, 2)],
                                 axis=1).astype(BF16)
        qk_gain = jnp.concatenate([jnp.tile(moba_q_gain[l], hb), jnp.tile(moba_k_gain[l], hb),
                                   jnp.tile(diff_q_gain[l], hb), jnp.tile(diff_k_gain[l], hb)])[None, :]
        g_attn = attn_norm_g[l][None, :]
        gates = _in_proj(x2d, g_attn, w_gate, gate_bias[l].reshape(1, -1), kind=KIND_GATE,
                         tm=tm, tn=tn, name="in_proj_gate")
        qk = _in_proj(x2d, g_attn, w_qk, qk_gain, kind=KIND_NORM, tm=tm, tn=tn,
                      name="in_proj_qk").reshape(b, s, -1)
        rest = _in_proj(x2d, g_attn, w_rest, jnp.ones((1, 5 * mw), F32), kind=KIND_PLAIN, tm=tm, tn=tn,
                        name="in_proj_rest").reshape(b, s, -1)

        oa = _moba(qk, rest, slopes_moba, n_heads=heads, q_col=0, k_col=hb, v_col=0)

        lam_init = 0.8 - 0.6 * math.exp(-0.3 * l)
        lam = (jnp.exp(jnp.sum(diff_lambda_q1[l] * diff_lambda_k1[l]))
               - jnp.exp(jnp.sum(diff_lambda_q2[l] * diff_lambda_k2[l])) + lam_init)
        ob = _diff_attention(qk, rest, slopes_diff, lam.reshape(1).astype(F32), diff_head_gain[l][None, :],
                             n_heads=diff_heads, q_col=2 * hb, k_col=3 * hb, v_col2=hb // 2,
                             out_scale=1.0 - lam_init)

        oc = _sb_attention(rest, n_heads=heads, q_col=2 * hb, k_col=3 * hb, v_col=4 * hb)

        w_router = jnp.concatenate([w_router_group[l], w_router_expert[l]], axis=1)
        w_router = jnp.pad(w_router, ((0, 0), (0, LANES - w_router.shape[1])))
        wr_hi, wr_lo = _split_bf16(w_router)
        x2d, h2d, route = _merge(
            oa.reshape(t, mw), ob.reshape(t, mw), oc.reshape(t, mw), gates, x2d,
            w_branch[l].astype(BF16), w_out[l].astype(BF16), ffn_norm_g[l][None, :],
            wr_hi, wr_lo, tm=256)

        weights = route[:, 0:EXPERT_TOPK]
        expert = route[:, EXPERT_TOPK:2 * EXPERT_TOPK].astype(jnp.int32)
        slot_token, assign_slot, block_expert, n_used = _dispatch_plan(expert)
        ys = _moe_experts(h2d[slot_token], block_expert, n_used,
                          w_expert_gate[l], w_expert_up[l], w_expert_down[l])
        y = (weights[:, 0:1] * ys[assign_slot[:, 0]] + weights[:, 1:2] * ys[assign_slot[:, 1]])
        x2d = x2d + y.astype(x2d.dtype)

    return x2d.reshape(b, s, d)
```

```python
import functools
import math

import jax
import jax.numpy as jnp
from jax import lax
from jax.experimental import pallas as pl
from jax.experimental.pallas import tpu as pltpu

F32 = jnp.float32
BF16 = jnp.bfloat16

HEAD_DIM = 128
N_BRANCH = 3
MOBA_BLOCK = 256
MOBA_TOPK = 3
N_GROUPS = 4
EXPERTS_PER_GROUP = 8
N_EXPERTS = N_GROUPS * EXPERTS_PER_GROUP
EXPERT_TOPK = 2
MOE_BLOCK = 256
RMS_EPS = 1e-6
ATT_BLOCK = 256
ATT_TILE = 512
MOBA_TILE = 256
HEADS_PER_STEP = 2
LANES = 128
MXU_DIM = 256
NEG_INF = float("-inf")
LOG2E = 1.4426950408889634
MIB = 1024 * 1024

KIND_PLAIN, KIND_NORM, KIND_GATE = 0, 1, 2


def _nt_dot(a, b):
    return lax.dot_general(a, b, (((1,), (1,)), ((), ())), preferred_element_type=F32)


def _dot(a, b):
    return jnp.dot(a, b, preferred_element_type=F32)


def _split_bf16(x):
    hi = x.astype(BF16)
    lo = (x - hi.astype(F32)).astype(BF16)
    return hi, lo


def _row_max(x):
    return jnp.max(x, axis=-1, keepdims=True)


def _row_sum(x):
    return jnp.sum(x, axis=-1, keepdims=True)


def _in_proj_kernel(src_ref, x_ref, g_ref, w_ref, vec_ref, o_ref, xn_ref, *, kind, chunk):
    del src_ref
    @pl.when(pl.program_id(1) == 0)
    def _():
        x = x_ref[...]
        ms = jnp.mean(x * x, axis=-1, keepdims=True)
        xn_ref[...] = (x * lax.rsqrt(ms + RMS_EPS) * g_ref[...]).astype(BF16)

    xn = xn_ref[...]
    for c in range(o_ref.shape[1] // chunk):
        cols = slice(c * chunk, (c + 1) * chunk)
        acc = _dot(xn, w_ref[:, cols])
        if kind == KIND_PLAIN:
            o_ref[:, cols] = acc.astype(o_ref.dtype)
        elif kind == KIND_GATE:
            z = acc + vec_ref[:, cols]
            o_ref[:, cols] = (0.5 * jnp.tanh(0.5 * z) + 0.5).astype(o_ref.dtype)
        else:
            for hd in range(chunk // HEAD_DIM):
                sub = slice(hd * HEAD_DIM, (hd + 1) * HEAD_DIM)
                hcols = slice(c * chunk + hd * HEAD_DIM, c * chunk + (hd + 1) * HEAD_DIM)
                blk = acc[:, sub]
                ms = jnp.mean(blk * blk, axis=-1, keepdims=True)
                o_ref[:, hcols] = (blk * lax.rsqrt(ms + RMS_EPS) * vec_ref[:, hcols]).astype(o_ref.dtype)


def _in_proj(x2d, g, w_all, layer, src_blocks, vec, *, kind, tm, tn, name):
    t, d = x2d.shape
    nj = len(src_blocks)
    kern = functools.partial(_in_proj_kernel, kind=kind, chunk=min(MXU_DIM, tn))
    return pl.pallas_call(
        kern,
        out_shape=jax.ShapeDtypeStruct((t, nj * tn), BF16),
        grid_spec=pltpu.PrefetchScalarGridSpec(
            num_scalar_prefetch=1,
            grid=(t // tm, nj),
            in_specs=[
                pl.BlockSpec((tm, d), lambda i, j, src: (i, 0)),
                pl.BlockSpec((1, d), lambda i, j, src: (0, 0)),
                pl.BlockSpec((None, d, tn), lambda i, j, src: (layer, 0, src[j])),
                pl.BlockSpec((1, tn), lambda i, j, src: (0, j)),
            ],
            out_specs=pl.BlockSpec((tm, tn), lambda i, j, src: (i, j)),
            scratch_shapes=[pltpu.VMEM((tm, d), BF16)],
        ),
        compiler_params=pltpu.CompilerParams(
            dimension_semantics=("parallel", "arbitrary"),
            vmem_limit_bytes=48 * MIB),
        name=name,
    )(jnp.asarray(src_blocks, jnp.int32), x2d, g, w_all, vec)


def _alibi_table(slopes, tq, s):
    r = jnp.arange(tq, dtype=jnp.int32)[:, None]
    j = jnp.arange(s, dtype=jnp.int32)[None, :]
    dist = (r - j + (s - tq)).astype(F32)
    return jnp.where(dist >= 0.0, (-LOG2E * slopes)[:, None, None] * dist[None], NEG_INF)


def _per_tile(qi, n_tiles, body):
    for c in range(n_tiles):
        pl.when(qi == c)(functools.partial(body, c))


def _head_cols(j, width=HEAD_DIM):
    return slice(j * width, (j + 1) * width)


def _moba_kernel(q_ref, qall_ref, k_ref, v_ref, bias_ref, o_ref, sel_ref,
                 *, hp, nb, blk, tq, topk, scale):
    qi = pl.program_id(2)
    s_len = nb * blk

    @pl.when(qi == 0)
    def _():
        row_id = lax.broadcasted_iota(jnp.int32, (LANES, HEAD_DIM), 0)
        lane = lax.broadcasted_iota(jnp.int32, (s_len, LANES), 1)
        own = lax.broadcasted_iota(jnp.int32, (s_len, LANES), 0) // blk
        lane_f = lane.astype(F32)
        for j in range(hp):
            hc = _head_cols(j)
            kmean = jnp.zeros((LANES, HEAD_DIM), F32)
            for n in range(nb):
                mean_n = jnp.sum(k_ref[n * blk:(n + 1) * blk, hc].astype(F32), axis=0, keepdims=True)
                kmean = jnp.where(row_id == n, mean_n * (1.0 / blk), kmean)
            km_hi, km_lo = _split_bf16(kmean)
            qall = qall_ref[:, hc]
            gate = _nt_dot(qall, km_hi) + _nt_dot(qall, km_lo)
            gate = jnp.where(lane < own, gate, NEG_INF)
            sel = jnp.where(lane == own, 1.0, 0.0)
            for _ in range(topk):
                gmax = _row_max(gate)
                first = jnp.min(jnp.where(gate == gmax, lane_f, float(LANES)), axis=-1, keepdims=True)
                pick = jnp.logical_and(lane_f == first, gmax > NEG_INF)
                sel = jnp.where(pick, 1.0, sel)
                gate = jnp.where(pick, NEG_INF, gate)
            sel_ref[j] = sel

    rows = pl.ds(pl.multiple_of(qi * tq, tq), tq)

    def tile(c):
        w = (c + 1) * tq
        for j in range(hp):
            hc = _head_cols(j)
            sel = sel_ref[j, rows, :]
            s = _nt_dot(q_ref[:, hc], k_ref[0:w, hc]) * scale + bias_ref[j, :, s_len - w:s_len]
            s = jnp.concatenate(
                [jnp.where(sel[:, n:n + 1] > 0.5, s[:, n * blk:(n + 1) * blk], NEG_INF)
                 for n in range(w // blk)], axis=1)
            p = jnp.exp2(s - _row_max(s))
            o = _dot(p.astype(BF16), v_ref[0:w, hc]) / _row_sum(p)
            o_ref[:, hc] = o.astype(o_ref.dtype)

    _per_tile(qi, s_len // tq, tile)


def _moba(qk, rest, slopes, *, n_heads, hp, q_col, k_col, v_col):
    b, s, _ = qk.shape
    blk = MOBA_BLOCK
    tq = min(MOBA_TILE, s)
    nb = s // blk
    assert s % blk == 0 and nb <= LANES and tq % blk == 0 and s % tq == 0
    assert n_heads % hp == 0 and q_col % hp == 0 and k_col % hp == 0 and v_col % hp == 0
    bias = _alibi_table(slopes, tq, s)
    kern = functools.partial(_moba_kernel, hp=hp, nb=nb, blk=blk, tq=tq, topk=min(MOBA_TOPK, nb),
                             scale=LOG2E * HEAD_DIM ** -0.5)
    wd = hp * HEAD_DIM
    return pl.pallas_call(
        kern,
        out_shape=jax.ShapeDtypeStruct((b, s, n_heads * HEAD_DIM), BF16),
        grid=(b, n_heads // hp, s // tq),
        in_specs=[
            pl.BlockSpec((None, tq, wd), lambda bi, h, qi: (bi, qi, q_col // hp + h)),
            pl.BlockSpec((None, s, wd), lambda bi, h, qi: (bi, 0, q_col // hp + h)),
            pl.BlockSpec((None, s, wd), lambda bi, h, qi: (bi, 0, k_col // hp + h)),
            pl.BlockSpec((None, s, wd), lambda bi, h, qi: (bi, 0, v_col // hp + h)),
            pl.BlockSpec((hp, tq, s), lambda bi, h, qi: (h, 0, 0)),
        ],
        out_specs=pl.BlockSpec((None, tq, wd), lambda bi, h, qi: (bi, qi, h)),
        scratch_shapes=[pltpu.VMEM((hp, s, LANES), F32)],
        compiler_params=pltpu.CompilerParams(
            dimension_semantics=("parallel", "parallel", "arbitrary"),
            vmem_limit_bytes=48 * MIB),
        name="moba",
    )(qk, qk, qk, rest, bias)


def _diff_kernel(lam_ref, q0_ref, q1_ref, k0_ref, k1_ref, v_ref, bias_ref, gain_ref, o_ref,
                 *, nq, tq, scale, out_scale):
    qi = pl.program_id(2)
    lam = lam_ref[0]
    s_len = nq * tq

    def tile(c):
        w = (c + 1) * tq
        bias = bias_ref[:, s_len - w:s_len]
        v = v_ref[0:w, :]

        def attend(q_ref, k_ref):
            s = _nt_dot(q_ref[...], k_ref[0:w, :]) * scale + bias
            p = jnp.exp2(s - _row_max(s))
            return _dot(p.astype(BF16), v) / _row_sum(p)

        o = attend(q0_ref, k0_ref) - lam * attend(q1_ref, k1_ref)
        ms = jnp.mean(o * o, axis=-1, keepdims=True)
        o_ref[...] = (o * lax.rsqrt(ms + RMS_EPS) * gain_ref[...] * out_scale).astype(o_ref.dtype)

    _per_tile(qi, nq, tile)


def _diff_attention(qk, rest, slopes, lam, head_gain, *, n_heads, q_col, k_col, v_col2, out_scale):
    b, s, _ = qk.shape
    tq = min(ATT_TILE, s)
    nq = s // tq
    dv = 2 * HEAD_DIM
    bias = _alibi_table(slopes, tq, s)
    kern = functools.partial(_diff_kernel, nq=nq, tq=tq, scale=LOG2E * HEAD_DIM ** -0.5,
                             out_scale=out_scale)
    qspec = lambda m: pl.BlockSpec((None, tq, HEAD_DIM),
                                   lambda bi, h, qi, lm: (bi, qi, q_col + 2 * h + m))
    kspec = lambda m: pl.BlockSpec((None, s, HEAD_DIM),
                                   lambda bi, h, qi, lm: (bi, 0, k_col + 2 * h + m))
    return pl.pallas_call(
        kern,
        out_shape=jax.ShapeDtypeStruct((b, s, n_heads * dv), BF16),
        grid_spec=pltpu.PrefetchScalarGridSpec(
            num_scalar_prefetch=1,
            grid=(b, n_heads, nq),
            in_specs=[
                qspec(0), qspec(1), kspec(0), kspec(1),
                pl.BlockSpec((None, s, dv), lambda bi, h, qi, lm: (bi, 0, v_col2 + h)),
                pl.BlockSpec((None, tq, s), lambda bi, h, qi, lm: (h, 0, 0)),
                pl.BlockSpec((1, dv), lambda bi, h, qi, lm: (0, 0)),
            ],
            out_specs=pl.BlockSpec((None, tq, dv), lambda bi, h, qi, lm: (bi, qi, h)),
        ),
        compiler_params=pltpu.CompilerParams(
            dimension_semantics=("parallel", "parallel", "arbitrary"),
            vmem_limit_bytes=48 * MIB),
        name="diff_attn",
    )(lam, qk, qk, qk, qk, rest, bias, head_gain)


def _sb_kernel(q_ref, k_ref, v_ref, o_ref, *, hp, nq, tq, kb, scale):
    qi = pl.program_id(2)
    row = lax.broadcasted_iota(jnp.int32, (tq, kb), 0)
    col = lax.broadcasted_iota(jnp.int32, (tq, kb), 1)
    urow = lax.broadcasted_iota(jnp.int32, (kb, kb), 0)
    ucol = lax.broadcasted_iota(jnp.int32, (kb, kb), 1)
    upper = jnp.where(urow > ucol, 1.0, 0.0).astype(BF16)

    def tile(c):
        carry = [None] * hp
        out = [None] * hp
        first_diag = c * tq // kb
        for n in range((c + 1) * tq // kb - 1, -1, -1):
            ks = slice(n * kb, (n + 1) * kb)
            strict = (col + n * kb) < (row + c * tq)
            for j in range(hp):
                hc = _head_cols(j)
                nz = _nt_dot(q_ref[:, hc], k_ref[ks, hc]) * (-scale)
                log_keep = jnp.minimum(nz, 0.0) - jnp.log(1.0 + jnp.exp(-jnp.abs(nz)))
                if n >= first_diag:
                    log_keep = jnp.where(strict, log_keep, 0.0)
                after = _dot(log_keep.astype(BF16), upper)
                if carry[j] is not None:
                    after = after + carry[j]
                a = jnp.exp(log_keep - nz + after)
                if n >= first_diag:
                    a = jnp.where(strict, a, 0.0)
                pv = _dot(a.astype(BF16), v_ref[ks, hc])
                out[j] = pv if out[j] is None else out[j] + pv
                if n > 0:
                    rs = _row_sum(log_keep)
                    carry[j] = rs if carry[j] is None else carry[j] + rs
        for j in range(hp):
            o_ref[:, _head_cols(j)] = out[j].astype(o_ref.dtype)

    _per_tile(qi, nq, tile)


def _sb_attention(rest, *, n_heads, hp, q_col, k_col, v_col):
    b, s, _ = rest.shape
    tq = min(ATT_TILE, s)
    kb = ATT_BLOCK
    nq = s // tq
    assert n_heads % hp == 0 and q_col % hp == 0 and k_col % hp == 0 and v_col % hp == 0
    assert s % tq == 0 and tq % kb == 0
    kern = functools.partial(_sb_kernel, hp=hp, nq=nq, tq=tq, kb=kb, scale=HEAD_DIM ** -0.5)
    wd = hp * HEAD_DIM
    return pl.pallas_call(
        kern,
        out_shape=jax.ShapeDtypeStruct((b, s, n_heads * HEAD_DIM), BF16),
        grid=(b, n_heads // hp, nq),
        in_specs=[
            pl.BlockSpec((None, tq, wd), lambda bi, h, qi: (bi, qi, q_col // hp + h)),
            pl.BlockSpec((None, s, wd), lambda bi, h, qi: (bi, 0, k_col // hp + h)),
            pl.BlockSpec((None, s, wd), lambda bi, h, qi: (bi, 0, v_col // hp + h)),
        ],
        out_specs=pl.BlockSpec((None, tq, wd), lambda bi, h, qi: (bi, qi, h)),
        compiler_params=pltpu.CompilerParams(
            dimension_semantics=("parallel", "parallel", "arbitrary"),
            vmem_limit_bytes=48 * MIB),
        name="stick_breaking",
    )(rest, rest, rest)


def _route_tile(lg):
    lane = lax.broadcasted_iota(jnp.int32, lg.shape, 1).astype(F32)
    first_of = lambda mask: jnp.min(jnp.where(mask, lane, float(LANES)), axis=-1, keepdims=True)

    is_group = lane < float(N_GROUPS)
    gl = jnp.where(is_group, lg, NEG_INF)
    ge = jnp.where(is_group, jnp.exp(gl - _row_max(gl)), 0.0)
    g_prob = ge / _row_sum(ge)
    g_top = _row_max(g_prob)
    g_idx = first_of(g_prob == g_top)

    e_lo = float(N_GROUPS) + float(EXPERTS_PER_GROUP) * g_idx
    in_group = jnp.logical_and(lane >= e_lo, lane < e_lo + float(EXPERTS_PER_GROUP))
    el = jnp.where(in_group, lg, NEG_INF)
    ee = jnp.where(in_group, jnp.exp(el - _row_max(el)), 0.0)
    e_prob = jnp.where(in_group, ee / _row_sum(ee), -1.0)
    top1 = _row_max(e_prob)
    idx1 = first_of(e_prob == top1)
    e_rest = jnp.where(lane == idx1, -1.0, e_prob)
    top2 = _row_max(e_rest)
    idx2 = first_of(e_rest == top2)
    denom = top1 + top2
    w1 = g_top * top1 / denom
    w2 = g_top * top2 / denom
    out = jnp.where(lane == 0.0, w1, 0.0)
    out = jnp.where(lane == 1.0, w2, out)
    out = jnp.where(lane == 2.0, idx1 - float(N_GROUPS), out)
    out = jnp.where(lane == 3.0, idx2 - float(N_GROUPS), out)
    return out


def _merge_kernel(oa_ref, ob_ref, oc_ref, ga_ref, gb_ref, gc_ref, x_ref, wb_ref, wo_ref, fg_ref,
                  wr_hi_ref, wr_lo_ref, xo_ref, h_ref, route_ref):
    mixed = ga_ref[...].astype(F32) * _dot(oa_ref[...], wb_ref[0])
    mixed += gb_ref[...].astype(F32) * _dot(ob_ref[...], wb_ref[1])
    mixed += gc_ref[...].astype(F32) * _dot(oc_ref[...], wb_ref[2])
    xn = x_ref[...] + _dot(mixed.astype(BF16), wo_ref[...])
    xo_ref[...] = xn
    ms = jnp.mean(xn * xn, axis=-1, keepdims=True)
    hn = xn * lax.rsqrt(ms + RMS_EPS) * fg_ref[...]
    h_hi, h_lo = _split_bf16(hn)
    h_ref[...] = h_hi
    logits = (_dot(h_hi, wr_hi_ref[...]) + _dot(h_hi, wr_lo_ref[...]) + _dot(h_lo, wr_hi_ref[...]))
    route_ref[...] = _route_tile(logits)


def _merge(oa, ob, oc, gates, x2d, wb, wo, fg, wr_hi, wr_lo, *, tm):
    t, d = x2d.shape
    mw = oa.shape[1]
    nr = wr_hi.shape[1]
    const = lambda *shape: pl.BlockSpec(shape, lambda i: (0,) * len(shape),
                                        pipeline_mode=pl.Buffered(1))
    gspec = lambda g: pl.BlockSpec((tm, d), lambda i: (i, g))
    return pl.pallas_call(
        _merge_kernel,
        out_shape=(jax.ShapeDtypeStruct((t, d), F32),
                   jax.ShapeDtypeStruct((t, d), BF16),
                   jax.ShapeDtypeStruct((t, nr), F32)),
        grid=(t // tm,),
        in_specs=[
            pl.BlockSpec((tm, mw), lambda i: (i, 0)),
            pl.BlockSpec((tm, mw), lambda i: (i, 0)),
            pl.BlockSpec((tm, mw), lambda i: (i, 0)),
            gspec(0), gspec(1), gspec(2),
            pl.BlockSpec((tm, d), lambda i: (i, 0)),
            const(N_BRANCH, mw, d),
            const(d, d),
            const(1, d),
            const(d, nr),
            const(d, nr),
        ],
        out_specs=(pl.BlockSpec((tm, d), lambda i: (i, 0)),
                   pl.BlockSpec((tm, d), lambda i: (i, 0)),
                   pl.BlockSpec((tm, nr), lambda i: (i, 0))),
        compiler_params=pltpu.CompilerParams(
            dimension_semantics=("parallel",),
            vmem_limit_bytes=56 * MIB),
        name="merge_out",
    )(oa, ob, oc, gates, gates, gates, x2d, wb, wo, fg, wr_hi, wr_lo)


def _moe_kernel(be_ref, nused_ref, xs_ref, wg_ref, wu_ref, wd_ref, ys_ref, wgb_ref, wub_ref, wdb_ref):
    i = pl.program_id(0)
    live = i < nused_ref[0]
    new_expert = jnp.logical_or(i == 0, be_ref[i] != be_ref[jnp.maximum(i - 1, 0)])

    @pl.when(jnp.logical_and(live, new_expert))
    def _():
        wgb_ref[...] = wg_ref[...].astype(BF16)
        wub_ref[...] = wu_ref[...].astype(BF16)
        wdb_ref[...] = wd_ref[...].astype(BF16)

    @pl.when(live)
    def _():
        x = xs_ref[...]
        g = _dot(x, wgb_ref[...])
        u = _dot(x, wub_ref[...])
        act = g * (1.0 / (1.0 + jnp.exp(-g))) * u
        ys_ref[...] = _dot(act.astype(BF16), wdb_ref[...]).astype(ys_ref.dtype)

    @pl.when(jnp.logical_not(live))
    def _():
        ys_ref[...] = jnp.zeros_like(ys_ref)


def _moe_experts(xs, block_expert, n_used, wg, wu, wd):
    n_slots, d = xs.shape
    ff = wg.shape[2]
    n_blocks = n_slots // MOE_BLOCK
    return pl.pallas_call(
        _moe_kernel,
        out_shape=jax.ShapeDtypeStruct((n_slots, d), F32),
        grid_spec=pltpu.PrefetchScalarGridSpec(
            num_scalar_prefetch=2,
            grid=(n_blocks,),
            in_specs=[
                pl.BlockSpec((MOE_BLOCK, d), lambda i, be, nu: (i, 0)),
                pl.BlockSpec((None, d, ff), lambda i, be, nu: (be[i], 0, 0)),
                pl.BlockSpec((None, d, ff), lambda i, be, nu: (be[i], 0, 0)),
                pl.BlockSpec((None, ff, d), lambda i, be, nu: (be[i], 0, 0)),
            ],
            out_specs=pl.BlockSpec((MOE_BLOCK, d), lambda i, be, nu: (i, 0)),
            scratch_shapes=[pltpu.VMEM((d, ff), BF16), pltpu.VMEM((d, ff), BF16),
                            pltpu.VMEM((ff, d), BF16)],
        ),
        compiler_params=pltpu.CompilerParams(
            dimension_semantics=("arbitrary",),
            vmem_limit_bytes=52 * MIB),
        name="moe_experts",
    )(block_expert, n_used, xs, wg, wu, wd)


def _dispatch_plan(expert):
    t = expert.shape[0]
    n_assign = t * EXPERT_TOPK
    flat_e = expert.reshape(-1)
    iota = jnp.arange(n_assign, dtype=jnp.int32)
    _, order = lax.sort((flat_e, iota), num_keys=1, is_stable=True)
    _, rank = lax.sort((order, iota), num_keys=1, is_stable=True)
    ids = jnp.arange(N_EXPERTS, dtype=jnp.int32)
    onehot = flat_e[:, None] == ids[None, :]
    counts = jnp.sum(onehot.astype(jnp.int32), axis=0)
    padded = ((counts + MOE_BLOCK - 1) // MOE_BLOCK) * MOE_BLOCK
    pend = jnp.cumsum(padded)
    pstart = pend - padded
    cstart = jnp.cumsum(counts) - counts
    n_blocks = -(-n_assign // MOE_BLOCK) + N_EXPERTS
    block_start = jnp.arange(n_blocks, dtype=jnp.int32) * MOE_BLOCK
    block_expert = jnp.minimum(jnp.sum((pend[None, :] <= block_start[:, None]).astype(jnp.int32), axis=1),
                               N_EXPERTS - 1)
    off = (block_start - pstart[block_expert])[:, None] + jnp.arange(MOE_BLOCK, dtype=jnp.int32)[None, :]
    pos = jnp.clip(cstart[block_expert][:, None] + off, 0, n_assign - 1).reshape(-1)
    slot_token = order[pos] // EXPERT_TOPK
    shift = jnp.sum(jnp.where(onehot, (pstart - cstart)[None, :], 0), axis=1)
    assign_slot = (rank + shift).reshape(t, EXPERT_TOPK)
    n_used = (pend[-1] // MOE_BLOCK).astype(jnp.int32).reshape(1)
    return slot_token, assign_slot, block_expert.astype(jnp.int32), n_used


def _alibi_slopes(n_heads):
    return jnp.exp2(-8.0 * jnp.arange(1, n_heads + 1, dtype=F32) / n_heads)


def kernel(x, attn_norm_g, w_in, gate_bias, moba_q_gain, moba_k_gain, diff_q_gain, diff_k_gain,
           diff_lambda_q1, diff_lambda_k1, diff_lambda_q2, diff_lambda_k2, diff_head_gain,
           w_branch, w_out, ffn_norm_g, w_router_group, w_router_expert,
           w_expert_gate, w_expert_up, w_expert_down):
    b, s, d = x.shape
    depth = w_in.shape[0]
    mw = d // 2
    heads = mw // HEAD_DIM
    diff_heads = mw // (2 * HEAD_DIM)
    hb = mw // HEAD_DIM
    t = b * s
    tn = min(1024, mw)
    tm = 512
    assert mw % tn == 0 and d % tn == 0 and t % tm == 0

    def seg_blocks(mixer, part):
        first = (mixer * 3 + part) * mw // tn
        return list(range(first, first + mw // tn))

    gate_blocks = list(range(9 * mw // tn, (9 * mw + N_BRANCH * d) // tn))
    qk_blocks = seg_blocks(0, 0) + seg_blocks(0, 1) + seg_blocks(1, 0) + seg_blocks(1, 1)
    rest_blocks = (seg_blocks(0, 2) + seg_blocks(1, 2) + seg_blocks(2, 0) + seg_blocks(2, 1)
                   + seg_blocks(2, 2))
    w_in_bf16 = w_in.astype(BF16)

    slopes_moba = _alibi_slopes(heads)
    slopes_diff = _alibi_slopes(diff_heads)
    x2d = x.reshape(t, d)

    for l in range(depth):
        qk_gain = jnp.concatenate([jnp.tile(moba_q_gain[l], hb), jnp.tile(moba_k_gain[l], hb),
                                   jnp.tile(diff_q_gain[l], hb), jnp.tile(diff_k_gain[l], hb)])[None, :]
        g_attn = attn_norm_g[l][None, :]
        gates = _in_proj(x2d, g_attn, w_in_bf16, l, gate_blocks, gate_bias[l].reshape(1, -1),
                         kind=KIND_GATE, tm=tm, tn=tn, name="in_proj_gate")
        qk = _in_proj(x2d, g_attn, w_in_bf16, l, qk_blocks, qk_gain, kind=KIND_NORM, tm=tm, tn=tn,
                      name="in_proj_qk").reshape(b, s, -1)
        rest = _in_proj(x2d, g_attn, w_in_bf16, l, rest_blocks, jnp.ones((1, 5 * mw), F32),
                        kind=KIND_PLAIN, tm=tm, tn=tn, name="in_proj_rest").reshape(b, s, -1)

        oa = _moba(qk, rest, slopes_moba, n_heads=heads, hp=HEADS_PER_STEP, q_col=0, k_col=hb, v_col=0)

        lam_init = 0.8 - 0.6 * math.exp(-0.3 * l)
        lam = (jnp.exp(jnp.sum(diff_lambda_q1[l] * diff_lambda_k1[l]))
               - jnp.exp(jnp.sum(diff_lambda_q2[l] * diff_lambda_k2[l])) + lam_init)
        ob = _diff_attention(qk, rest, slopes_diff, lam.reshape(1).astype(F32), diff_head_gain[l][None, :],
                             n_heads=diff_heads, q_col=2 * hb, k_col=3 * hb, v_col2=hb // 2,
                             out_scale=1.0 - lam_init)

        oc = _sb_attention(rest, n_heads=heads, hp=HEADS_PER_STEP, q_col=2 * hb, k_col=3 * hb,
                           v_col=4 * hb)

        w_router = jnp.concatenate([w_router_group[l], w_router_expert[l]], axis=1)
        w_router = jnp.pad(w_router, ((0, 0), (0, LANES - w_router.shape[1])))
        wr_hi, wr_lo = _split_bf16(w_router)
        x2d, h2d, route = _merge(
            oa.reshape(t, mw), ob.reshape(t, mw), oc.reshape(t, mw), gates, x2d,
            w_branch[l].astype(BF16), w_out[l].astype(BF16), ffn_norm_g[l][None, :],
            wr_hi, wr_lo, tm=256)

        weights = route[:, 0:EXPERT_TOPK]
        expert = route[:, EXPERT_TOPK:2 * EXPERT_TOPK].astype(jnp.int32)
        slot_token, assign_slot, block_expert, n_used = _dispatch_plan(expert)
        ys = _moe_experts(h2d[slot_token], block_expert, n_used,
                          w_expert_gate[l], w_expert_up[l], w_expert_down[l])
        y = (weights[:, 0:1] * ys[assign_slot[:, 0]] + weights[:, 1:2] * ys[assign_slot[:, 1]])
        x2d = x2d + y.astype(x2d.dtype)

    return x2d.reshape(b, s, d)
```

```python
import functools
import math

import jax
import jax.numpy as jnp
from jax import lax
from jax.experimental import pallas as pl
from jax.experimental.pallas import tpu as pltpu

F32 = jnp.float32
BF16 = jnp.bfloat16

HEAD_DIM = 128
N_BRANCH = 3
MOBA_BLOCK = 256
MOBA_TOPK = 3
N_GROUPS = 4
EXPERTS_PER_GROUP = 8
N_EXPERTS = N_GROUPS * EXPERTS_PER_GROUP
EXPERT_TOPK = 2
MOE_BLOCK = 256
RMS_EPS = 1e-6
ATT_BLOCK = 256
ATT_TILE = 512
MOBA_TILE = 256
HEADS_PER_STEP = 2
LANES = 128
BF16_SUBLANES = 16
MXU_DIM = 256
NEG_INF = float("-inf")
LOG2E = 1.4426950408889634
MIB = 1024 * 1024

KIND_PLAIN, KIND_NORM, KIND_GATE = 0, 1, 2


def _nt_dot(a, b):
    return lax.dot_general(a, b, (((1,), (1,)), ((), ())), preferred_element_type=F32)


def _dot(a, b):
    return jnp.dot(a, b, preferred_element_type=F32)


def _split_bf16(x):
    hi = x.astype(BF16)
    lo = (x - hi.astype(F32)).astype(BF16)
    return hi, lo


def _row_max(x):
    return jnp.max(x, axis=-1, keepdims=True)


def _row_sum(x):
    return jnp.sum(x, axis=-1, keepdims=True)


def _in_proj_kernel(src_ref, x_ref, g_ref, w_ref, vec_ref, o_ref, xn_ref, *, kind, chunk):
    del src_ref
    @pl.when(pl.program_id(1) == 0)
    def _():
        x = x_ref[...]
        ms = jnp.mean(x * x, axis=-1, keepdims=True)
        xn_ref[...] = (x * lax.rsqrt(ms + RMS_EPS) * g_ref[...]).astype(BF16)

    xn = xn_ref[...]
    for c in range(o_ref.shape[1] // chunk):
        cols = slice(c * chunk, (c + 1) * chunk)
        acc = _dot(xn, w_ref[:, cols])
        if kind == KIND_PLAIN:
            o_ref[:, cols] = acc.astype(o_ref.dtype)
        elif kind == KIND_GATE:
            z = acc + vec_ref[:, cols]
            o_ref[:, cols] = (0.5 * jnp.tanh(0.5 * z) + 0.5).astype(o_ref.dtype)
        else:
            for hd in range(chunk // HEAD_DIM):
                sub = slice(hd * HEAD_DIM, (hd + 1) * HEAD_DIM)
                hcols = slice(c * chunk + hd * HEAD_DIM, c * chunk + (hd + 1) * HEAD_DIM)
                blk = acc[:, sub]
                ms = jnp.mean(blk * blk, axis=-1, keepdims=True)
                o_ref[:, hcols] = (blk * lax.rsqrt(ms + RMS_EPS) * vec_ref[:, hcols]).astype(o_ref.dtype)


def _in_proj(x2d, g, w_all, layer, src_blocks, vec, *, kind, tm, tn, name):
    t, d = x2d.shape
    nj = len(src_blocks)
    kern = functools.partial(_in_proj_kernel, kind=kind, chunk=min(MXU_DIM, tn))
    return pl.pallas_call(
        kern,
        out_shape=jax.ShapeDtypeStruct((t, nj * tn), BF16),
        grid_spec=pltpu.PrefetchScalarGridSpec(
            num_scalar_prefetch=1,
            grid=(t // tm, nj),
            in_specs=[
                pl.BlockSpec((tm, d), lambda i, j, src: (i, 0)),
                pl.BlockSpec((1, d), lambda i, j, src: (0, 0)),
                pl.BlockSpec((None, d, tn), lambda i, j, src: (layer, 0, src[j])),
                pl.BlockSpec((1, tn), lambda i, j, src: (0, j)),
            ],
            out_specs=pl.BlockSpec((tm, tn), lambda i, j, src: (i, j)),
            scratch_shapes=[pltpu.VMEM((tm, d), BF16)],
        ),
        compiler_params=pltpu.CompilerParams(
            dimension_semantics=("parallel", "arbitrary"),
            vmem_limit_bytes=48 * MIB),
        name=name,
    )(jnp.asarray(src_blocks, jnp.int32), x2d, g, w_all, vec)


def _alibi_table(slopes, tq, s):
    r = jnp.arange(tq, dtype=jnp.int32)[:, None]
    j = jnp.arange(s, dtype=jnp.int32)[None, :]
    dist = (r - j + (s - tq)).astype(F32)
    return jnp.where(dist >= 0.0, (-LOG2E * slopes)[:, None, None] * dist[None], NEG_INF)


def _per_tile(qi, n_tiles, body):
    for c in range(n_tiles):
        pl.when(qi == c)(functools.partial(body, c))


def _head_cols(j, width=HEAD_DIM):
    return slice(j * width, (j + 1) * width)


def _moba_kernel(q_ref, qall_ref, k_ref, v_ref, bias_ref, o_ref, sel_ref,
                 *, hp, nb, blk, tq, topk, scale):
    qi = pl.program_id(2)
    s_len = nb * blk

    @pl.when(qi == 0)
    def _():
        nbp = -(-nb // BF16_SUBLANES) * BF16_SUBLANES
        row_id = lax.broadcasted_iota(jnp.int32, (nbp, HEAD_DIM), 0)
        blk_id = lax.broadcasted_iota(jnp.int32, (nbp, s_len), 0)
        own = lax.broadcasted_iota(jnp.int32, (nbp, s_len), 1) // blk
        blk_f = blk_id.astype(F32)
        for j in range(hp):
            hc = _head_cols(j)
            kmean = jnp.zeros((nbp, HEAD_DIM), F32)
            for n in range(nb):
                mean_n = jnp.sum(k_ref[n * blk:(n + 1) * blk, hc].astype(F32), axis=0, keepdims=True)
                kmean = jnp.where(row_id == n, mean_n * (1.0 / blk), kmean)
            km_hi, km_lo = _split_bf16(kmean)
            qall = qall_ref[:, hc]
            gate = _nt_dot(km_hi, qall) + _nt_dot(km_lo, qall)
            gate = jnp.where(blk_id < own, gate, NEG_INF)
            sel = jnp.where(blk_id == own, 1.0, 0.0)
            for _ in range(topk):
                gmax = jnp.max(gate, axis=0, keepdims=True)
                first = jnp.min(jnp.where(gate == gmax, blk_f, float(nbp)), axis=0, keepdims=True)
                pick = jnp.logical_and(blk_f == first, gmax > NEG_INF)
                sel = jnp.where(pick, 1.0, sel)
                gate = jnp.where(pick, NEG_INF, gate)
            sel = jnp.concatenate([sel, jnp.zeros((LANES - nbp, s_len), F32)], axis=0)
            sel_ref[j] = sel.T

    rows = pl.ds(pl.multiple_of(qi * tq, tq), tq)

    def tile(c):
        w = (c + 1) * tq
        for j in range(hp):
            hc = _head_cols(j)
            sel = sel_ref[j, rows, :]
            s = _nt_dot(q_ref[:, hc], k_ref[0:w, hc]) * scale + bias_ref[j, :, s_len - w:s_len]
            s = jnp.concatenate(
                [jnp.where(sel[:, n:n + 1] > 0.5, s[:, n * blk:(n + 1) * blk], NEG_INF)
                 for n in range(w // blk)], axis=1)
            p = jnp.exp2(s - _row_max(s))
            o = _dot(p.astype(BF16), v_ref[0:w, hc]) / _row_sum(p)
            o_ref[:, hc] = o.astype(o_ref.dtype)

    _per_tile(qi, s_len // tq, tile)


def _moba(qk, rest, slopes, *, n_heads, hp, q_col, k_col, v_col):
    b, s, _ = qk.shape
    blk = MOBA_BLOCK
    tq = min(MOBA_TILE, s)
    nb = s // blk
    assert s % blk == 0 and nb <= LANES and tq % blk == 0 and s % tq == 0
    assert n_heads % hp == 0 and q_col % hp == 0 and k_col % hp == 0 and v_col % hp == 0
    bias = _alibi_table(slopes, tq, s)
    kern = functools.partial(_moba_kernel, hp=hp, nb=nb, blk=blk, tq=tq, topk=min(MOBA_TOPK, nb),
                             scale=LOG2E * HEAD_DIM ** -0.5)
    wd = hp * HEAD_DIM
    return pl.pallas_call(
        kern,
        out_shape=jax.ShapeDtypeStruct((b, s, n_heads * HEAD_DIM), BF16),
        grid=(b, n_heads // hp, s // tq),
        in_specs=[
            pl.BlockSpec((None, tq, wd), lambda bi, h, qi: (bi, qi, q_col // hp + h)),
            pl.BlockSpec((None, s, wd), lambda bi, h, qi: (bi, 0, q_col // hp + h)),
            pl.BlockSpec((None, s, wd), lambda bi, h, qi: (bi, 0, k_col // hp + h)),
            pl.BlockSpec((None, s, wd), lambda bi, h, qi: (bi, 0, v_col // hp + h)),
            pl.BlockSpec((hp, tq, s), lambda bi, h, qi: (h, 0, 0)),
        ],
        out_specs=pl.BlockSpec((None, tq, wd), lambda bi, h, qi: (bi, qi, h)),
        scratch_shapes=[pltpu.VMEM((hp, s, LANES), F32)],
        compiler_params=pltpu.CompilerParams(
            dimension_semantics=("parallel", "parallel", "arbitrary"),
            vmem_limit_bytes=48 * MIB),
        name="moba",
    )(qk, qk, qk, rest, bias)


def _diff_kernel(lam_ref, q0_ref, q1_ref, k0_ref, k1_ref, v_ref, bias_ref, gain_ref, o_ref,
                 *, nq, tq, scale, out_scale):
    qi = pl.program_id(2)
    lam = lam_ref[0]
    s_len = nq * tq

    def tile(c):
        w = (c + 1) * tq
        bias = bias_ref[:, s_len - w:s_len]
        v = v_ref[0:w, :]

        def attend(q_ref, k_ref):
            s = _nt_dot(q_ref[...], k_ref[0:w, :]) * scale + bias
            p = jnp.exp2(s - _row_max(s))
            return _dot(p.astype(BF16), v) / _row_sum(p)

        o = attend(q0_ref, k0_ref) - lam * attend(q1_ref, k1_ref)
        ms = jnp.mean(o * o, axis=-1, keepdims=True)
        o_ref[...] = (o * lax.rsqrt(ms + RMS_EPS) * gain_ref[...] * out_scale).astype(o_ref.dtype)

    _per_tile(qi, nq, tile)


def _diff_attention(qk, rest, slopes, lam, head_gain, *, n_heads, q_col, k_col, v_col2, out_scale):
    b, s, _ = qk.shape
    tq = min(ATT_TILE, s)
    nq = s // tq
    dv = 2 * HEAD_DIM
    bias = _alibi_table(slopes, tq, s)
    kern = functools.partial(_diff_kernel, nq=nq, tq=tq, scale=LOG2E * HEAD_DIM ** -0.5,
                             out_scale=out_scale)
    qspec = lambda m: pl.BlockSpec((None, tq, HEAD_DIM),
                                   lambda bi, h, qi, lm: (bi, qi, q_col + 2 * h + m))
    kspec = lambda m: pl.BlockSpec((None, s, HEAD_DIM),
                                   lambda bi, h, qi, lm: (bi, 0, k_col + 2 * h + m))
    return pl.pallas_call(
        kern,
        out_shape=jax.ShapeDtypeStruct((b, s, n_heads * dv), BF16),
        grid_spec=pltpu.PrefetchScalarGridSpec(
            num_scalar_prefetch=1,
            grid=(b, n_heads, nq),
            in_specs=[
                qspec(0), qspec(1), kspec(0), kspec(1),
                pl.BlockSpec((None, s, dv), lambda bi, h, qi, lm: (bi, 0, v_col2 + h)),
                pl.BlockSpec((None, tq, s), lambda bi, h, qi, lm: (h, 0, 0)),
                pl.BlockSpec((1, dv), lambda bi, h, qi, lm: (0, 0)),
            ],
            out_specs=pl.BlockSpec((None, tq, dv), lambda bi, h, qi, lm: (bi, qi, h)),
        ),
        compiler_params=pltpu.CompilerParams(
            dimension_semantics=("parallel", "parallel", "arbitrary"),
            vmem_limit_bytes=48 * MIB),
        name="diff_attn",
    )(lam, qk, qk, qk, qk, rest, bias, head_gain)


def _sb_kernel(q_ref, k_ref, v_ref, o_ref, *, hp, nq, tq, kb, scale):
    qi = pl.program_id(2)
    row = lax.broadcasted_iota(jnp.int32, (tq, kb), 0)
    col = lax.broadcasted_iota(jnp.int32, (tq, kb), 1)
    urow = lax.broadcasted_iota(jnp.int32, (kb, kb), 0)
    ucol = lax.broadcasted_iota(jnp.int32, (kb, kb), 1)
    upper = jnp.where(urow > ucol, 1.0, 0.0).astype(BF16)

    def tile(c):
        carry = [None] * hp
        out = [None] * hp
        first_diag = c * tq // kb
        for n in range((c + 1) * tq // kb - 1, -1, -1):
            ks = slice(n * kb, (n + 1) * kb)
            strict = (col + n * kb) < (row + c * tq)
            for j in range(hp):
                hc = _head_cols(j)
                nz = _nt_dot(q_ref[:, hc], k_ref[ks, hc]) * (-scale)
                log_keep = jnp.minimum(nz, 0.0) - jnp.log(1.0 + jnp.exp(-jnp.abs(nz)))
                if n >= first_diag:
                    log_keep = jnp.where(strict, log_keep, 0.0)
                after = _dot(log_keep.astype(BF16), upper)
                if carry[j] is not None:
                    after = after + carry[j]
                a = jnp.exp(log_keep - nz + after)
                if n >= first_diag:
                    a = jnp.where(strict, a, 0.0)
                pv = _dot(a.astype(BF16), v_ref[ks, hc])
                out[j] = pv if out[j] is None else out[j] + pv
                if n > 0:
                    rs = _row_sum(log_keep)
                    carry[j] = rs if carry[j] is None else carry[j] + rs
        for j in range(hp):
            o_ref[:, _head_cols(j)] = out[j].astype(o_ref.dtype)

    _per_tile(qi, nq, tile)


def _sb_attention(rest, *, n_heads, hp, q_col, k_col, v_col):
    b, s, _ = rest.shape
    tq = min(ATT_TILE, s)
    kb = ATT_BLOCK
    nq = s // tq
    assert n_heads % hp == 0 and q_col % hp == 0 and k_col % hp == 0 and v_col % hp == 0
    assert s % tq == 0 and tq % kb == 0
    kern = functools.partial(_sb_kernel, hp=hp, nq=nq, tq=tq, kb=kb, scale=HEAD_DIM ** -0.5)
    wd = hp * HEAD_DIM
    return pl.pallas_call(
        kern,
        out_shape=jax.ShapeDtypeStruct((b, s, n_heads * HEAD_DIM), BF16),
        grid=(b, n_heads // hp, nq),
        in_specs=[
            pl.BlockSpec((None, tq, wd), lambda bi, h, qi: (bi, qi, q_col // hp + h)),
            pl.BlockSpec((None, s, wd), lambda bi, h, qi: (bi, 0, k_col // hp + h)),
            pl.BlockSpec((None, s, wd), lambda bi, h, qi: (bi, 0, v_col // hp + h)),
        ],
        out_specs=pl.BlockSpec((None, tq, wd), lambda bi, h, qi: (bi, qi, h)),
        compiler_params=pltpu.CompilerParams(
            dimension_semantics=("parallel", "parallel", "arbitrary"),
            vmem_limit_bytes=48 * MIB),
        name="stick_breaking",
    )(rest, rest, rest)


def _route_tile(lg):
    lane = lax.broadcasted_iota(jnp.int32, lg.shape, 1).astype(F32)
    first_of = lambda mask: jnp.min(jnp.where(mask, lane, float(LANES)), axis=-1, keepdims=True)

    is_group = lane < float(N_GROUPS)
    gl = jnp.where(is_group, lg, NEG_INF)
    ge = jnp.where(is_group, jnp.exp(gl - _row_max(gl)), 0.0)
    g_prob = ge / _row_sum(ge)
    g_top = _row_max(g_prob)
    g_idx = first_of(g_prob == g_top)

    e_lo = float(N_GROUPS) + float(EXPERTS_PER_GROUP) * g_idx
    in_group = jnp.logical_and(lane >= e_lo, lane < e_lo + float(EXPERTS_PER_GROUP))
    el = jnp.where(in_group, lg, NEG_INF)
    ee = jnp.where(in_group, jnp.exp(el - _row_max(el)), 0.0)
    e_prob = jnp.where(in_group, ee / _row_sum(ee), -1.0)
    top1 = _row_max(e_prob)
    idx1 = first_of(e_prob == top1)
    e_rest = jnp.where(lane == idx1, -1.0, e_prob)
    top2 = _row_max(e_rest)
    idx2 = first_of(e_rest == top2)
    denom = top1 + top2
    w1 = g_top * top1 / denom
    w2 = g_top * top2 / denom
    out = jnp.where(lane == 0.0, w1, 0.0)
    out = jnp.where(lane == 1.0, w2, out)
    out = jnp.where(lane == 2.0, idx1 - float(N_GROUPS), out)
    out = jnp.where(lane == 3.0, idx2 - float(N_GROUPS), out)
    return out


def _merge_kernel(oa_ref, ob_ref, oc_ref, ga_ref, gb_ref, gc_ref, x_ref, wb_ref, wo_ref, fg_ref,
                  wr_hi_ref, wr_lo_ref, xo_ref, h_ref, route_ref):
    mixed = ga_ref[...].astype(F32) * _dot(oa_ref[...], wb_ref[0])
    mixed += gb_ref[...].astype(F32) * _dot(ob_ref[...], wb_ref[1])
    mixed += gc_ref[...].astype(F32) * _dot(oc_ref[...], wb_ref[2])
    xn = x_ref[...] + _dot(mixed.astype(BF16), wo_ref[...])
    xo_ref[...] = xn
    ms = jnp.mean(xn * xn, axis=-1, keepdims=True)
    hn = xn * lax.rsqrt(ms + RMS_EPS) * fg_ref[...]
    h_hi, h_lo = _split_bf16(hn)
    h_ref[...] = h_hi
    logits = (_dot(h_hi, wr_hi_ref[...]) + _dot(h_hi, wr_lo_ref[...]) + _dot(h_lo, wr_hi_ref[...]))
    route_ref[...] = _route_tile(logits)


def _merge(oa, ob, oc, gates, x2d, wb, wo, fg, wr_hi, wr_lo, *, tm):
    t, d = x2d.shape
    mw = oa.shape[1]
    nr = wr_hi.shape[1]
    const = lambda *shape: pl.BlockSpec(shape, lambda i: (0,) * len(shape),
                                        pipeline_mode=pl.Buffered(1))
    gspec = lambda g: pl.BlockSpec((tm, d), lambda i: (i, g))
    return pl.pallas_call(
        _merge_kernel,
        out_shape=(jax.ShapeDtypeStruct((t, d), F32),
                   jax.ShapeDtypeStruct((t, d), BF16),
                   jax.ShapeDtypeStruct((t, nr), F32)),
        grid=(t // tm,),
        in_specs=[
            pl.BlockSpec((tm, mw), lambda i: (i, 0)),
            pl.BlockSpec((tm, mw), lambda i: (i, 0)),
            pl.BlockSpec((tm, mw), lambda i: (i, 0)),
            gspec(0), gspec(1), gspec(2),
            pl.BlockSpec((tm, d), lambda i: (i, 0)),
            const(N_BRANCH, mw, d),
            const(d, d),
            const(1, d),
            const(d, nr),
            const(d, nr),
        ],
        out_specs=(pl.BlockSpec((tm, d), lambda i: (i, 0)),
                   pl.BlockSpec((tm, d), lambda i: (i, 0)),
                   pl.BlockSpec((tm, nr), lambda i: (i, 0))),
        compiler_params=pltpu.CompilerParams(
            dimension_semantics=("parallel",),
            vmem_limit_bytes=56 * MIB),
        name="merge_out",
    )(oa, ob, oc, gates, gates, gates, x2d, wb, wo, fg, wr_hi, wr_lo)


def _moe_kernel(be_ref, nused_ref, xs_ref, wg_ref, wu_ref, wd_ref, ys_ref, wgb_ref, wub_ref, wdb_ref):
    i = pl.program_id(0)
    live = i < nused_ref[0]
    new_expert = jnp.logical_or(i == 0, be_ref[i] != be_ref[jnp.maximum(i - 1, 0)])

    @pl.when(jnp.logical_and(live, new_expert))
    def _():
        wgb_ref[...] = wg_ref[...].astype(BF16)
        wub_ref[...] = wu_ref[...].astype(BF16)
        wdb_ref[...] = wd_ref[...].astype(BF16)

    @pl.when(live)
    def _():
        x = xs_ref[...]
        g = _dot(x, wgb_ref[...])
        u = _dot(x, wub_ref[...])
        act = g * (1.0 / (1.0 + jnp.exp(-g))) * u
        ys_ref[...] = _dot(act.astype(BF16), wdb_ref[...]).astype(ys_ref.dtype)

    @pl.when(jnp.logical_not(live))
    def _():
        ys_ref[...] = jnp.zeros_like(ys_ref)


def _moe_experts(xs, block_expert, n_used, wg, wu, wd, layer):
    n_slots, d = xs.shape
    ff = wg.shape[3]
    n_blocks = n_slots // MOE_BLOCK
    return pl.pallas_call(
        _moe_kernel,
        out_shape=jax.ShapeDtypeStruct((n_slots, d), F32),
        grid_spec=pltpu.PrefetchScalarGridSpec(
            num_scalar_prefetch=2,
            grid=(n_blocks,),
            in_specs=[
                pl.BlockSpec((MOE_BLOCK, d), lambda i, be, nu: (i, 0)),
                pl.BlockSpec((None, None, d, ff), lambda i, be, nu: (layer, be[i], 0, 0)),
                pl.BlockSpec((None, None, d, ff), lambda i, be, nu: (layer, be[i], 0, 0)),
                pl.BlockSpec((None, None, ff, d), lambda i, be, nu: (layer, be[i], 0, 0)),
            ],
            out_specs=pl.BlockSpec((MOE_BLOCK, d), lambda i, be, nu: (i, 0)),
            scratch_shapes=[pltpu.VMEM((d, ff), BF16), pltpu.VMEM((d, ff), BF16),
                            pltpu.VMEM((ff, d), BF16)],
        ),
        compiler_params=pltpu.CompilerParams(
            dimension_semantics=("arbitrary",),
            vmem_limit_bytes=52 * MIB),
        name="moe_experts",
    )(block_expert, n_used, xs, wg, wu, wd)


def _dispatch_plan(expert):
    t = expert.shape[0]
    n_assign = t * EXPERT_TOPK
    flat_e = expert.reshape(-1)
    iota = jnp.arange(n_assign, dtype=jnp.int32)
    _, order = lax.sort((flat_e, iota), num_keys=1, is_stable=True)
    _, rank = lax.sort((order, iota), num_keys=1, is_stable=True)
    ids = jnp.arange(N_EXPERTS, dtype=jnp.int32)
    onehot = flat_e[:, None] == ids[None, :]
    counts = jnp.sum(onehot.astype(jnp.int32), axis=0)
    padded = ((counts + MOE_BLOCK - 1) // MOE_BLOCK) * MOE_BLOCK
    pend = jnp.cumsum(padded)
    pstart = pend - padded
    cstart = jnp.cumsum(counts) - counts
    n_blocks = -(-n_assign // MOE_BLOCK) + N_EXPERTS
    block_start = jnp.arange(n_blocks, dtype=jnp.int32) * MOE_BLOCK
    block_expert = jnp.minimum(jnp.sum((pend[None, :] <= block_start[:, None]).astype(jnp.int32), axis=1),
                               N_EXPERTS - 1)
    off = (block_start - pstart[block_expert])[:, None] + jnp.arange(MOE_BLOCK, dtype=jnp.int32)[None, :]
    pos = jnp.clip(cstart[block_expert][:, None] + off, 0, n_assign - 1).reshape(-1)
    slot_token = order[pos] // EXPERT_TOPK
    shift = jnp.sum(jnp.where(onehot, (pstart - cstart)[None, :], 0), axis=1)
    assign_slot = (rank + shift).reshape(t, EXPERT_TOPK)
    n_used = (pend[-1] // MOE_BLOCK).astype(jnp.int32).reshape(1)
    return slot_token, assign_slot, block_expert.astype(jnp.int32), n_used


def _alibi_slopes(n_heads):
    return jnp.exp2(-8.0 * jnp.arange(1, n_heads + 1, dtype=F32) / n_heads)


def kernel(x, attn_norm_g, w_in, gate_bias, moba_q_gain, moba_k_gain, diff_q_gain, diff_k_gain,
           diff_lambda_q1, diff_lambda_k1, diff_lambda_q2, diff_lambda_k2, diff_head_gain,
           w_branch, w_out, ffn_norm_g, w_router_group, w_router_expert,
           w_expert_gate, w_expert_up, w_expert_down):
    b, s, d = x.shape
    depth = w_in.shape[0]
    mw = d // 2
    heads = mw // HEAD_DIM
    diff_heads = mw // (2 * HEAD_DIM)
    hb = mw // HEAD_DIM
    t = b * s
    tn = min(1024, mw)
    tm = min(1024, t)
    assert mw % tn == 0 and d % tn == 0 and t % tm == 0

    def seg_blocks(mixer, part):
        first = (mixer * 3 + part) * mw // tn
        return list(range(first, first + mw // tn))

    gate_blocks = list(range(9 * mw // tn, (9 * mw + N_BRANCH * d) // tn))
    qk_blocks = seg_blocks(0, 0) + seg_blocks(0, 1) + seg_blocks(1, 0) + seg_blocks(1, 1)
    rest_blocks = (seg_blocks(0, 2) + seg_blocks(1, 2) + seg_blocks(2, 0) + seg_blocks(2, 1)
                   + seg_blocks(2, 2))
    w_in_bf16 = w_in.astype(BF16)

    slopes_moba = _alibi_slopes(heads)
    slopes_diff = _alibi_slopes(diff_heads)
    x2d = x.reshape(t, d)

    for l in range(depth):
        qk_gain = jnp.concatenate([jnp.tile(moba_q_gain[l], hb), jnp.tile(moba_k_gain[l], hb),
                                   jnp.tile(diff_q_gain[l], hb), jnp.tile(diff_k_gain[l], hb)])[None, :]
        g_attn = attn_norm_g[l][None, :]
        gates = _in_proj(x2d, g_attn, w_in_bf16, l, gate_blocks, gate_bias[l].reshape(1, -1),
                         kind=KIND_GATE, tm=tm, tn=tn, name="in_proj_gate")
        qk = _in_proj(x2d, g_attn, w_in_bf16, l, qk_blocks, qk_gain, kind=KIND_NORM, tm=tm, tn=tn,
                      name="in_proj_qk").reshape(b, s, -1)
        rest = _in_proj(x2d, g_attn, w_in_bf16, l, rest_blocks, jnp.ones((1, 5 * mw), F32),
                        kind=KIND_PLAIN, tm=tm, tn=tn, name="in_proj_rest").reshape(b, s, -1)

        oa = _moba(qk, rest, slopes_moba, n_heads=heads, hp=HEADS_PER_STEP, q_col=0, k_col=hb, v_col=0)

        lam_init = 0.8 - 0.6 * math.exp(-0.3 * l)
        lam = (jnp.exp(jnp.sum(diff_lambda_q1[l] * diff_lambda_k1[l]))
               - jnp.exp(jnp.sum(diff_lambda_q2[l] * diff_lambda_k2[l])) + lam_init)
        ob = _diff_attention(qk, rest, slopes_diff, lam.reshape(1).astype(F32), diff_head_gain[l][None, :],
                             n_heads=diff_heads, q_col=2 * hb, k_col=3 * hb, v_col2=hb // 2,
                             out_scale=1.0 - lam_init)

        oc = _sb_attention(rest, n_heads=heads, hp=HEADS_PER_STEP, q_col=2 * hb, k_col=3 * hb,
                           v_col=4 * hb)

        w_router = jnp.concatenate([w_router_group[l], w_router_expert[l]], axis=1)
        w_router = jnp.pad(w_router, ((0, 0), (0, LANES - w_router.shape[1])))
        wr_hi, wr_lo = _split_bf16(w_router)
        x2d, h2d, route = _merge(
            oa.reshape(t, mw), ob.reshape(t, mw), oc.reshape(t, mw), gates, x2d,
            w_branch[l].astype(BF16), w_out[l].astype(BF16), ffn_norm_g[l][None, :],
            wr_hi, wr_lo, tm=256)

        weights = route[:, 0:EXPERT_TOPK]
        expert = route[:, EXPERT_TOPK:2 * EXPERT_TOPK].astype(jnp.int32)
        slot_token, assign_slot, block_expert, n_used = _dispatch_plan(expert)
        ys = _moe_experts(h2d[slot_token], block_expert, n_used,
                          w_expert_gate, w_expert_up, w_expert_down, l)
        y = (weights[:, 0:1] * ys[assign_slot[:, 0]] + weights[:, 1:2] * ys[assign_slot[:, 1]])
        x2d = x2d + y.astype(x2d.dtype)

    return x2d.reshape(b, s, d)
```

```python
import functools
import math

import jax
import jax.numpy as jnp
from jax import lax
from jax.experimental import pallas as pl
from jax.experimental.pallas import tpu as pltpu

F32 = jnp.float32
BF16 = jnp.bfloat16

HEAD_DIM = 128
N_BRANCH = 3
MOBA_BLOCK = 256
MOBA_TOPK = 3
N_GROUPS = 4
EXPERTS_PER_GROUP = 8
N_EXPERTS = N_GROUPS * EXPERTS_PER_GROUP
EXPERT_TOPK = 2
MOE_BLOCK = 256
RMS_EPS = 1e-6
ATT_BLOCK = 256
ATT_TILE = 512
MOBA_TILE = 256
BATCH_CHUNKS = 2
HEADS_PER_STEP = 4
LANES = 128
BF16_SUBLANES = 16
MXU_DIM = 256
NEG_INF = float("-inf")
LOG2E = 1.4426950408889634
MIB = 1024 * 1024

KIND_PLAIN, KIND_NORM, KIND_GATE = 0, 1, 2


def _nt_dot(a, b):
    return lax.dot_general(a, b, (((1,), (1,)), ((), ())), preferred_element_type=F32)


def _dot(a, b):
    return jnp.dot(a, b, preferred_element_type=F32)


def _split_bf16(x):
    hi = x.astype(BF16)
    lo = (x - hi.astype(F32)).astype(BF16)
    return hi, lo


def _row_max(x):
    return jnp.max(x, axis=-1, keepdims=True)


def _row_sum(x):
    return jnp.sum(x, axis=-1, keepdims=True)


def _in_proj_kernel(src_ref, x_ref, g_ref, w_ref, vec_ref, o_ref, xn_ref, *, kind, chunk):
    del src_ref
    @pl.when(pl.program_id(1) == 0)
    def _():
        x = x_ref[...]
        ms = jnp.mean(x * x, axis=-1, keepdims=True)
        xn_ref[...] = (x * lax.rsqrt(ms + RMS_EPS) * g_ref[...]).astype(BF16)

    xn = xn_ref[...]
    for c in range(o_ref.shape[1] // chunk):
        cols = slice(c * chunk, (c + 1) * chunk)
        acc = _dot(xn, w_ref[:, cols])
        if kind == KIND_PLAIN:
            o_ref[:, cols] = acc.astype(o_ref.dtype)
        elif kind == KIND_GATE:
            z = acc + vec_ref[:, cols]
            o_ref[:, cols] = (0.5 * jnp.tanh(0.5 * z) + 0.5).astype(o_ref.dtype)
        else:
            for hd in range(chunk // HEAD_DIM):
                sub = slice(hd * HEAD_DIM, (hd + 1) * HEAD_DIM)
                hcols = slice(c * chunk + hd * HEAD_DIM, c * chunk + (hd + 1) * HEAD_DIM)
                blk = acc[:, sub]
                ms = jnp.mean(blk * blk, axis=-1, keepdims=True)
                o_ref[:, hcols] = (blk * lax.rsqrt(ms + RMS_EPS) * vec_ref[:, hcols]).astype(o_ref.dtype)


def _in_proj(x2d, g, w_all, layer, src_blocks, vec, *, kind, tm, tn, name):
    t, d = x2d.shape
    nj = len(src_blocks)
    kern = functools.partial(_in_proj_kernel, kind=kind, chunk=min(MXU_DIM, tn))
    return pl.pallas_call(
        kern,
        out_shape=jax.ShapeDtypeStruct((t, nj * tn), BF16),
        grid_spec=pltpu.PrefetchScalarGridSpec(
            num_scalar_prefetch=1,
            grid=(t // tm, nj),
            in_specs=[
                pl.BlockSpec((tm, d), lambda i, j, src: (i, 0)),
                pl.BlockSpec((1, d), lambda i, j, src: (0, 0)),
                pl.BlockSpec((None, d, tn), lambda i, j, src: (layer, 0, src[j])),
                pl.BlockSpec((1, tn), lambda i, j, src: (0, j)),
            ],
            out_specs=pl.BlockSpec((tm, tn), lambda i, j, src: (i, j)),
            scratch_shapes=[pltpu.VMEM((tm, d), BF16)],
        ),
        compiler_params=pltpu.CompilerParams(
            dimension_semantics=("parallel", "arbitrary"),
            vmem_limit_bytes=48 * MIB),
        name=name,
    )(jnp.asarray(src_blocks, jnp.int32), x2d, g, w_all, vec)


def _alibi_table(slopes, tq, s):
    r = jnp.arange(tq, dtype=jnp.int32)[:, None]
    j = jnp.arange(s, dtype=jnp.int32)[None, :]
    dist = (r - j + (s - tq)).astype(F32)
    return jnp.where(dist >= 0.0, (-LOG2E * slopes)[:, None, None] * dist[None], NEG_INF)


def _per_tile(qi, n_tiles, body):
    for c in range(n_tiles):
        pl.when(qi == c)(functools.partial(body, c))


def _head_cols(j, width=HEAD_DIM):
    return slice(j * width, (j + 1) * width)


def _moba_kernel(q_ref, qall_ref, k_ref, v_ref, bias_ref, o_ref, sel_ref,
                 *, hp, nb, blk, tq, topk, scale):
    qi = pl.program_id(2)
    s_len = nb * blk

    @pl.when(qi == 0)
    def _():
        nbp = -(-nb // BF16_SUBLANES) * BF16_SUBLANES
        row_id = lax.broadcasted_iota(jnp.int32, (nbp, HEAD_DIM), 0)
        blk_id = lax.broadcasted_iota(jnp.int32, (nbp, s_len), 0)
        own = lax.broadcasted_iota(jnp.int32, (nbp, s_len), 1) // blk
        blk_f = blk_id.astype(F32)
        for j in range(hp):
            hc = _head_cols(j)
            kmean = jnp.zeros((nbp, HEAD_DIM), F32)
            for n in range(nb):
                mean_n = jnp.sum(k_ref[n * blk:(n + 1) * blk, hc].astype(F32), axis=0, keepdims=True)
                kmean = jnp.where(row_id == n, mean_n * (1.0 / blk), kmean)
            km_hi, km_lo = _split_bf16(kmean)
            qall = qall_ref[:, hc]
            gate = _nt_dot(km_hi, qall) + _nt_dot(km_lo, qall)
            gate = jnp.where(blk_id < own, gate, NEG_INF)
            sel = jnp.where(blk_id == own, 1.0, 0.0)
            for _ in range(topk):
                gmax = jnp.max(gate, axis=0, keepdims=True)
                first = jnp.min(jnp.where(gate == gmax, blk_f, float(nbp)), axis=0, keepdims=True)
                pick = jnp.logical_and(blk_f == first, gmax > NEG_INF)
                sel = jnp.where(pick, 1.0, sel)
                gate = jnp.where(pick, NEG_INF, gate)
            sel = jnp.concatenate([sel, jnp.zeros((LANES - nbp, s_len), F32)], axis=0)
            sel_ref[j] = sel.T

    rows = pl.ds(pl.multiple_of(qi * tq, tq), tq)

    def tile(c):
        w = (c + 1) * tq
        for j in range(hp):
            hc = _head_cols(j)
            sel = sel_ref[j, rows, :]
            s = _nt_dot(q_ref[:, hc], k_ref[0:w, hc]) * scale + bias_ref[j, :, s_len - w:s_len]
            s = jnp.concatenate(
                [jnp.where(sel[:, n:n + 1] > 0.5, s[:, n * blk:(n + 1) * blk], NEG_INF)
                 for n in range(w // blk)], axis=1)
            p = jnp.exp2(s - _row_max(s))
            o = _dot(p.astype(BF16), v_ref[0:w, hc]) / _row_sum(p)
            o_ref[:, hc] = o.astype(o_ref.dtype)

    _per_tile(qi, s_len // tq, tile)


def _moba(qk, rest, slopes, *, n_heads, hp, q_col, k_col, v_col):
    b, s, _ = qk.shape
    blk = MOBA_BLOCK
    tq = min(MOBA_TILE, s)
    nb = s // blk
    assert s % blk == 0 and nb <= LANES and tq % blk == 0 and s % tq == 0
    assert n_heads % hp == 0 and q_col % hp == 0 and k_col % hp == 0 and v_col % hp == 0
    bias = _alibi_table(slopes, tq, s)
    kern = functools.partial(_moba_kernel, hp=hp, nb=nb, blk=blk, tq=tq, topk=min(MOBA_TOPK, nb),
                             scale=LOG2E * HEAD_DIM ** -0.5)
    wd = hp * HEAD_DIM
    return pl.pallas_call(
        kern,
        out_shape=jax.ShapeDtypeStruct((b, s, n_heads * HEAD_DIM), BF16),
        grid=(b, n_heads // hp, s // tq),
        in_specs=[
            pl.BlockSpec((None, tq, wd), lambda bi, h, qi: (bi, qi, q_col // hp + h)),
            pl.BlockSpec((None, s, wd), lambda bi, h, qi: (bi, 0, q_col // hp + h)),
            pl.BlockSpec((None, s, wd), lambda bi, h, qi: (bi, 0, k_col // hp + h)),
            pl.BlockSpec((None, s, wd), lambda bi, h, qi: (bi, 0, v_col // hp + h)),
            pl.BlockSpec((hp, tq, s), lambda bi, h, qi: (h, 0, 0)),
        ],
        out_specs=pl.BlockSpec((None, tq, wd), lambda bi, h, qi: (bi, qi, h)),
        scratch_shapes=[pltpu.VMEM((hp, s, LANES), F32)],
        compiler_params=pltpu.CompilerParams(
            dimension_semantics=("parallel", "parallel", "arbitrary"),
            vmem_limit_bytes=48 * MIB),
        name="moba",
    )(qk, qk, qk, rest, bias)


def _diff_kernel(lam_ref, q0_ref, q1_ref, k0_ref, k1_ref, v_ref, bias_ref, gain_ref, o_ref,
                 *, nq, tq, scale, out_scale):
    qi = pl.program_id(2)
    lam = lam_ref[0]
    s_len = nq * tq

    def tile(c):
        w = (c + 1) * tq
        bias = bias_ref[:, s_len - w:s_len]
        v = v_ref[0:w, :]

        def attend(q_ref, k_ref):
            s = _nt_dot(q_ref[...], k_ref[0:w, :]) * scale + bias
            p = jnp.exp2(s - _row_max(s))
            return _dot(p.astype(BF16), v) / _row_sum(p)

        o = attend(q0_ref, k0_ref) - lam * attend(q1_ref, k1_ref)
        ms = jnp.mean(o * o, axis=-1, keepdims=True)
        o_ref[...] = (o * lax.rsqrt(ms + RMS_EPS) * gain_ref[...] * out_scale).astype(o_ref.dtype)

    _per_tile(qi, nq, tile)


def _diff_attention(qk, rest, slopes, lam, head_gain, *, n_heads, q_col, k_col, v_col2, out_scale):
    b, s, _ = qk.shape
    tq = min(ATT_TILE, s)
    nq = s // tq
    dv = 2 * HEAD_DIM
    bias = _alibi_table(slopes, tq, s)
    kern = functools.partial(_diff_kernel, nq=nq, tq=tq, scale=LOG2E * HEAD_DIM ** -0.5,
                             out_scale=out_scale)
    qspec = lambda m: pl.BlockSpec((None, tq, HEAD_DIM),
                                   lambda bi, h, qi, lm: (bi, qi, q_col + 2 * h + m))
    kspec = lambda m: pl.BlockSpec((None, s, HEAD_DIM),
                                   lambda bi, h, qi, lm: (bi, 0, k_col + 2 * h + m))
    return pl.pallas_call(
        kern,
        out_shape=jax.ShapeDtypeStruct((b, s, n_heads * dv), BF16),
        grid_spec=pltpu.PrefetchScalarGridSpec(
            num_scalar_prefetch=1,
            grid=(b, n_heads, nq),
            in_specs=[
                qspec(0), qspec(1), kspec(0), kspec(1),
                pl.BlockSpec((None, s, dv), lambda bi, h, qi, lm: (bi, 0, v_col2 + h)),
                pl.BlockSpec((None, tq, s), lambda bi, h, qi, lm: (h, 0, 0)),
                pl.BlockSpec((1, dv), lambda bi, h, qi, lm: (0, 0)),
            ],
            out_specs=pl.BlockSpec((None, tq, dv), lambda bi, h, qi, lm: (bi, qi, h)),
        ),
        compiler_params=pltpu.CompilerParams(
            dimension_semantics=("parallel", "parallel", "arbitrary"),
            vmem_limit_bytes=48 * MIB),
        name="diff_attn",
    )(lam, qk, qk, qk, qk, rest, bias, head_gain)


def _sb_kernel(q_ref, k_ref, v_ref, o_ref, *, hp, nq, tq, kb, scale):
    qi = pl.program_id(2)
    row = lax.broadcasted_iota(jnp.int32, (tq, kb), 0)
    col = lax.broadcasted_iota(jnp.int32, (tq, kb), 1)
    urow = lax.broadcasted_iota(jnp.int32, (kb, kb), 0)
    ucol = lax.broadcasted_iota(jnp.int32, (kb, kb), 1)
    upper = jnp.where(urow > ucol, 1.0, 0.0).astype(BF16)

    def tile(c):
        carry = [None] * hp
        out = [None] * hp
        first_diag = c * tq // kb
        for n in range((c + 1) * tq // kb - 1, -1, -1):
            ks = slice(n * kb, (n + 1) * kb)
            strict = (col + n * kb) < (row + c * tq)
            for j in range(hp):
                hc = _head_cols(j)
                nz = _nt_dot(q_ref[:, hc], k_ref[ks, hc]) * (-scale)
                log_keep = jnp.minimum(nz, 0.0) - jnp.log(1.0 + jnp.exp(-jnp.abs(nz)))
                if n >= first_diag:
                    log_keep = jnp.where(strict, log_keep, 0.0)
                after = _dot(log_keep.astype(BF16), upper)
                if carry[j] is not None:
                    after = after + carry[j]
                a = jnp.exp(log_keep - nz + after)
                if n >= first_diag:
                    a = jnp.where(strict, a, 0.0)
                pv = _dot(a.astype(BF16), v_ref[ks, hc])
                out[j] = pv if out[j] is None else out[j] + pv
                if n > 0:
                    rs = _row_sum(log_keep)
                    carry[j] = rs if carry[j] is None else carry[j] + rs
        for j in range(hp):
            o_ref[:, _head_cols(j)] = out[j].astype(o_ref.dtype)

    _per_tile(qi, nq, tile)


def _sb_attention(rest, *, n_heads, hp, q_col, k_col, v_col):
    b, s, _ = rest.shape
    tq = min(ATT_TILE, s)
    kb = ATT_BLOCK
    nq = s // tq
    assert n_heads % hp == 0 and q_col % hp == 0 and k_col % hp == 0 and v_col % hp == 0
    assert s % tq == 0 and tq % kb == 0
    kern = functools.partial(_sb_kernel, hp=hp, nq=nq, tq=tq, kb=kb, scale=HEAD_DIM ** -0.5)
    wd = hp * HEAD_DIM
    return pl.pallas_call(
        kern,
        out_shape=jax.ShapeDtypeStruct((b, s, n_heads * HEAD_DIM), BF16),
        grid=(b, n_heads // hp, nq),
        in_specs=[
            pl.BlockSpec((None, tq, wd), lambda bi, h, qi: (bi, qi, q_col // hp + h)),
            pl.BlockSpec((None, s, wd), lambda bi, h, qi: (bi, 0, k_col // hp + h)),
            pl.BlockSpec((None, s, wd), lambda bi, h, qi: (bi, 0, v_col // hp + h)),
        ],
        out_specs=pl.BlockSpec((None, tq, wd), lambda bi, h, qi: (bi, qi, h)),
        compiler_params=pltpu.CompilerParams(
            dimension_semantics=("parallel", "parallel", "arbitrary"),
            vmem_limit_bytes=48 * MIB),
        name="stick_breaking",
    )(rest, rest, rest)


def _route_tile(lg):
    lane = lax.broadcasted_iota(jnp.int32, lg.shape, 1).astype(F32)
    first_of = lambda mask: jnp.min(jnp.where(mask, lane, float(LANES)), axis=-1, keepdims=True)

    is_group = lane < float(N_GROUPS)
    gl = jnp.where(is_group, lg, NEG_INF)
    ge = jnp.where(is_group, jnp.exp(gl - _row_max(gl)), 0.0)
    g_prob = ge / _row_sum(ge)
    g_top = _row_max(g_prob)
    g_idx = first_of(g_prob == g_top)

    e_lo = float(N_GROUPS) + float(EXPERTS_PER_GROUP) * g_idx
    in_group = jnp.logical_and(lane >= e_lo, lane < e_lo + float(EXPERTS_PER_GROUP))
    el = jnp.where(in_group, lg, NEG_INF)
    ee = jnp.where(in_group, jnp.exp(el - _row_max(el)), 0.0)
    e_prob = jnp.where(in_group, ee / _row_sum(ee), -1.0)
    top1 = _row_max(e_prob)
    idx1 = first_of(e_prob == top1)
    e_rest = jnp.where(lane == idx1, -1.0, e_prob)
    top2 = _row_max(e_rest)
    idx2 = first_of(e_rest == top2)
    denom = top1 + top2
    w1 = g_top * top1 / denom
    w2 = g_top * top2 / denom
    out = jnp.where(lane == 0.0, w1, 0.0)
    out = jnp.where(lane == 1.0, w2, out)
    out = jnp.where(lane == 2.0, idx1 - float(N_GROUPS), out)
    out = jnp.where(lane == 3.0, idx2 - float(N_GROUPS), out)
    return out


def _merge_kernel(oa_ref, ob_ref, oc_ref, ga_ref, gb_ref, gc_ref, x_ref, wb_ref, wo_ref, fg_ref,
                  wr_hi_ref, wr_lo_ref, xo_ref, h_ref, route_ref):
    mixed = ga_ref[...].astype(F32) * _dot(oa_ref[...], wb_ref[0])
    mixed += gb_ref[...].astype(F32) * _dot(ob_ref[...], wb_ref[1])
    mixed += gc_ref[...].astype(F32) * _dot(oc_ref[...], wb_ref[2])
    xn = x_ref[...] + _dot(mixed.astype(BF16), wo_ref[...])
    xo_ref[...] = xn
    ms = jnp.mean(xn * xn, axis=-1, keepdims=True)
    hn = xn * lax.rsqrt(ms + RMS_EPS) * fg_ref[...]
    h_hi, h_lo = _split_bf16(hn)
    h_ref[...] = h_hi
    logits = (_dot(h_hi, wr_hi_ref[...]) + _dot(h_hi, wr_lo_ref[...]) + _dot(h_lo, wr_hi_ref[...]))
    route_ref[...] = _route_tile(logits)


def _merge(oa, ob, oc, gates, x2d, wb, wo, fg, wr_hi, wr_lo, *, tm):
    t, d = x2d.shape
    mw = oa.shape[1]
    nr = wr_hi.shape[1]
    const = lambda *shape: pl.BlockSpec(shape, lambda i: (0,) * len(shape),
                                        pipeline_mode=pl.Buffered(1))
    gspec = lambda g: pl.BlockSpec((tm, d), lambda i: (i, g))
    return pl.pallas_call(
        _merge_kernel,
        out_shape=(jax.ShapeDtypeStruct((t, d), F32),
                   jax.ShapeDtypeStruct((t, d), BF16),
                   jax.ShapeDtypeStruct((t, nr), F32)),
        grid=(t // tm,),
        in_specs=[
            pl.BlockSpec((tm, mw), lambda i: (i, 0)),
            pl.BlockSpec((tm, mw), lambda i: (i, 0)),
            pl.BlockSpec((tm, mw), lambda i: (i, 0)),
            gspec(0), gspec(1), gspec(2),
            pl.BlockSpec((tm, d), lambda i: (i, 0)),
            const(N_BRANCH, mw, d),
            const(d, d),
            const(1, d),
            const(d, nr),
            const(d, nr),
        ],
        out_specs=(pl.BlockSpec((tm, d), lambda i: (i, 0)),
                   pl.BlockSpec((tm, d), lambda i: (i, 0)),
                   pl.BlockSpec((tm, nr), lambda i: (i, 0))),
        compiler_params=pltpu.CompilerParams(
            dimension_semantics=("parallel",),
            vmem_limit_bytes=56 * MIB),
        name="merge_out",
    )(oa, ob, oc, gates, gates, gates, x2d, wb, wo, fg, wr_hi, wr_lo)


def _moe_kernel(be_ref, nused_ref, xs_ref, wg_ref, wu_ref, wd_ref, ys_ref, wgb_ref, wub_ref, wdb_ref):
    i = pl.program_id(0)
    live = i < nused_ref[0]
    new_expert = jnp.logical_or(i == 0, be_ref[i] != be_ref[jnp.maximum(i - 1, 0)])

    @pl.when(jnp.logical_and(live, new_expert))
    def _():
        wgb_ref[...] = wg_ref[...].astype(BF16)
        wub_ref[...] = wu_ref[...].astype(BF16)
        wdb_ref[...] = wd_ref[...].astype(BF16)

    @pl.when(live)
    def _():
        x = xs_ref[...]
        g = _dot(x, wgb_ref[...])
        u = _dot(x, wub_ref[...])
        act = g * (1.0 / (1.0 + jnp.exp(-g))) * u
        ys_ref[...] = _dot(act.astype(BF16), wdb_ref[...]).astype(ys_ref.dtype)

    @pl.when(jnp.logical_not(live))
    def _():
        ys_ref[...] = jnp.zeros_like(ys_ref)


def _moe_experts(xs, block_expert, n_used, wg, wu, wd, layer):
    n_slots, d = xs.shape
    ff = wg.shape[3]
    n_blocks = n_slots // MOE_BLOCK
    return pl.pallas_call(
        _moe_kernel,
        out_shape=jax.ShapeDtypeStruct((n_slots, d), F32),
        grid_spec=pltpu.PrefetchScalarGridSpec(
            num_scalar_prefetch=2,
            grid=(n_blocks,),
            in_specs=[
                pl.BlockSpec((MOE_BLOCK, d), lambda i, be, nu: (i, 0)),
                pl.BlockSpec((None, None, d, ff), lambda i, be, nu: (layer, be[i], 0, 0)),
                pl.BlockSpec((None, None, d, ff), lambda i, be, nu: (layer, be[i], 0, 0)),
                pl.BlockSpec((None, None, ff, d), lambda i, be, nu: (layer, be[i], 0, 0)),
            ],
            out_specs=pl.BlockSpec((MOE_BLOCK, d), lambda i, be, nu: (i, 0)),
            scratch_shapes=[pltpu.VMEM((d, ff), BF16), pltpu.VMEM((d, ff), BF16),
                            pltpu.VMEM((ff, d), BF16)],
        ),
        compiler_params=pltpu.CompilerParams(
            dimension_semantics=("arbitrary",),
            vmem_limit_bytes=52 * MIB),
        name="moe_experts",
    )(block_expert, n_used, xs, wg, wu, wd)


def _dispatch_plan(expert):
    t = expert.shape[0]
    n_assign = t * EXPERT_TOPK
    flat_e = expert.reshape(-1)
    iota = jnp.arange(n_assign, dtype=jnp.int32)
    _, order = lax.sort((flat_e, iota), num_keys=1, is_stable=True)
    _, rank = lax.sort((order, iota), num_keys=1, is_stable=True)
    ids = jnp.arange(N_EXPERTS, dtype=jnp.int32)
    onehot = flat_e[:, None] == ids[None, :]
    counts = jnp.sum(onehot.astype(jnp.int32), axis=0)
    padded = ((counts + MOE_BLOCK - 1) // MOE_BLOCK) * MOE_BLOCK
    pend = jnp.cumsum(padded)
    pstart = pend - padded
    cstart = jnp.cumsum(counts) - counts
    n_blocks = -(-n_assign // MOE_BLOCK) + N_EXPERTS
    block_start = jnp.arange(n_blocks, dtype=jnp.int32) * MOE_BLOCK
    block_expert = jnp.minimum(jnp.sum((pend[None, :] <= block_start[:, None]).astype(jnp.int32), axis=1),
                               N_EXPERTS - 1)
    off = (block_start - pstart[block_expert])[:, None] + jnp.arange(MOE_BLOCK, dtype=jnp.int32)[None, :]
    pos = jnp.clip(cstart[block_expert][:, None] + off, 0, n_assign - 1).reshape(-1)
    slot_token = order[pos] // EXPERT_TOPK
    shift = jnp.sum(jnp.where(onehot, (pstart - cstart)[None, :], 0), axis=1)
    assign_slot = (rank + shift).reshape(t, EXPERT_TOPK)
    n_used = (pend[-1] // MOE_BLOCK).astype(jnp.int32).reshape(1)
    return slot_token, assign_slot, block_expert.astype(jnp.int32), n_used


def _alibi_slopes(n_heads):
    return jnp.exp2(-8.0 * jnp.arange(1, n_heads + 1, dtype=F32) / n_heads)


def kernel(x, attn_norm_g, w_in, gate_bias, moba_q_gain, moba_k_gain, diff_q_gain, diff_k_gain,
           diff_lambda_q1, diff_lambda_k1, diff_lambda_q2, diff_lambda_k2, diff_head_gain,
           w_branch, w_out, ffn_norm_g, w_router_group, w_router_expert,
           w_expert_gate, w_expert_up, w_expert_down):
    b, s, d = x.shape
    depth = w_in.shape[0]
    mw = d // 2
    heads = mw // HEAD_DIM
    diff_heads = mw // (2 * HEAD_DIM)
    hb = mw // HEAD_DIM
    t = b * s
    tn = min(1024, mw)
    tm = min(1024, t)
    assert mw % tn == 0 and d % tn == 0 and t % tm == 0

    def seg_blocks(mixer, part):
        first = (mixer * 3 + part) * mw // tn
        return list(range(first, first + mw // tn))

    gate_blocks = list(range(9 * mw // tn, (9 * mw + N_BRANCH * d) // tn))
    qk_blocks = seg_blocks(0, 0) + seg_blocks(0, 1) + seg_blocks(1, 0) + seg_blocks(1, 1)
    rest_blocks = (seg_blocks(0, 2) + seg_blocks(1, 2) + seg_blocks(2, 0) + seg_blocks(2, 1)
                   + seg_blocks(2, 2))
    w_in_bf16 = w_in.astype(BF16)

    slopes_moba = _alibi_slopes(heads)
    slopes_diff = _alibi_slopes(diff_heads)
    w_branch_bf16 = w_branch.astype(BF16)
    w_out_bf16 = w_out.astype(BF16)

    def layer(xc, l):
        tc = xc.shape[0]
        bc = tc // s
        qk_gain = jnp.concatenate([jnp.tile(moba_q_gain[l], hb), jnp.tile(moba_k_gain[l], hb),
                                   jnp.tile(diff_q_gain[l], hb), jnp.tile(diff_k_gain[l], hb)])[None, :]
        g_attn = attn_norm_g[l][None, :]
        gates = _in_proj(xc, g_attn, w_in_bf16, l, gate_blocks, gate_bias[l].reshape(1, -1),
                         kind=KIND_GATE, tm=tm, tn=tn, name="in_proj_gate")
        qk = _in_proj(xc, g_attn, w_in_bf16, l, qk_blocks, qk_gain, kind=KIND_NORM, tm=tm, tn=tn,
                      name="in_proj_qk").reshape(bc, s, -1)
        rest = _in_proj(xc, g_attn, w_in_bf16, l, rest_blocks, jnp.ones((1, 5 * mw), F32),
                        kind=KIND_PLAIN, tm=tm, tn=tn, name="in_proj_rest").reshape(bc, s, -1)

        hp = min(HEADS_PER_STEP, heads)
        oa = _moba(qk, rest, slopes_moba, n_heads=heads, hp=hp, q_col=0, k_col=hb, v_col=0)

        lam_init = 0.8 - 0.6 * math.exp(-0.3 * l)
        lam = (jnp.exp(jnp.sum(diff_lambda_q1[l] * diff_lambda_k1[l]))
               - jnp.exp(jnp.sum(diff_lambda_q2[l] * diff_lambda_k2[l])) + lam_init)
        ob = _diff_attention(qk, rest, slopes_diff, lam.reshape(1).astype(F32), diff_head_gain[l][None, :],
                             n_heads=diff_heads, q_col=2 * hb, k_col=3 * hb, v_col2=hb // 2,
                             out_scale=1.0 - lam_init)

        oc = _sb_attention(rest, n_heads=heads, hp=hp, q_col=2 * hb, k_col=3 * hb,
                           v_col=4 * hb)

        w_router = jnp.concatenate([w_router_group[l], w_router_expert[l]], axis=1)
        w_router = jnp.pad(w_router, ((0, 0), (0, LANES - w_router.shape[1])))
        wr_hi, wr_lo = _split_bf16(w_router)
        x_mid, h2d, route = _merge(
            oa.reshape(tc, mw), ob.reshape(tc, mw), oc.reshape(tc, mw), gates, xc,
            w_branch_bf16[l], w_out_bf16[l], ffn_norm_g[l][None, :], wr_hi, wr_lo, tm=256)

        weights = route[:, 0:EXPERT_TOPK]
        expert = route[:, EXPERT_TOPK:2 * EXPERT_TOPK].astype(jnp.int32)
        slot_token, assign_slot, block_expert, n_used = _dispatch_plan(expert)
        ys = _moe_experts(h2d[slot_token], block_expert, n_used,
                          w_expert_gate, w_expert_up, w_expert_down, l)
        y = (weights[:, 0:1] * ys[assign_slot[:, 0]] + weights[:, 1:2] * ys[assign_slot[:, 1]])
        return x_mid + y.astype(x_mid.dtype)

    n_chunks = BATCH_CHUNKS if b % BATCH_CHUNKS == 0 else 1
    tc = t // n_chunks
    tm = min(tm, tc)
    x2d = x.reshape(t, d)
    chunks = [x2d[c * tc:(c + 1) * tc] for c in range(n_chunks)]
    for l in range(depth):
        chunks = [layer(xc, l) for xc in chunks]
    return jnp.concatenate(chunks, axis=0).reshape(b, s, d)
```

```python
import functools
import math

import jax
import jax.numpy as jnp
from jax import lax
from jax.experimental import pallas as pl
from jax.experimental.pallas import tpu as pltpu

F32 = jnp.float32
BF16 = jnp.bfloat16

HEAD_DIM = 128
N_BRANCH = 3
MOBA_BLOCK = 256
MOBA_TOPK = 3
N_GROUPS = 4
EXPERTS_PER_GROUP = 8
N_EXPERTS = N_GROUPS * EXPERTS_PER_GROUP
EXPERT_TOPK = 2
MOE_BLOCK = 256
RMS_EPS = 1e-6
ATT_BLOCK = 256
ATT_TILE = 512
MOBA_TILE = 256
BATCH_CHUNKS = 1
HEADS_PER_STEP = 4
LANES = 128
BF16_SUBLANES = 16
MXU_DIM = 256
NEG_INF = float("-inf")
LOG2E = 1.4426950408889634
MIB = 1024 * 1024

KIND_PLAIN, KIND_NORM, KIND_GATE = 0, 1, 2


def _nt_dot(a, b):
    return lax.dot_general(a, b, (((1,), (1,)), ((), ())), preferred_element_type=F32)


def _dot(a, b):
    return jnp.dot(a, b, preferred_element_type=F32)


def _split_bf16(x):
    hi = x.astype(BF16)
    lo = (x - hi.astype(F32)).astype(BF16)
    return hi, lo


def _row_max(x):
    return jnp.max(x, axis=-1, keepdims=True)


def _row_sum(x):
    return jnp.sum(x, axis=-1, keepdims=True)


def _in_proj_kernel(src_ref, x_ref, g_ref, w_ref, vec_ref, o_ref, xn_ref, *, kind, chunk):
    del src_ref
    @pl.when(pl.program_id(1) == 0)
    def _():
        x = x_ref[...]
        ms = jnp.mean(x * x, axis=-1, keepdims=True)
        xn_ref[...] = (x * lax.rsqrt(ms + RMS_EPS) * g_ref[...]).astype(BF16)

    xn = xn_ref[...]
    for c in range(o_ref.shape[1] // chunk):
        cols = slice(c * chunk, (c + 1) * chunk)
        acc = _dot(xn, w_ref[:, cols])
        if kind == KIND_PLAIN:
            o_ref[:, cols] = acc.astype(o_ref.dtype)
        elif kind == KIND_GATE:
            z = acc + vec_ref[:, cols]
            o_ref[:, cols] = (0.5 * jnp.tanh(0.5 * z) + 0.5).astype(o_ref.dtype)
        else:
            for hd in range(chunk // HEAD_DIM):
                sub = slice(hd * HEAD_DIM, (hd + 1) * HEAD_DIM)
                hcols = slice(c * chunk + hd * HEAD_DIM, c * chunk + (hd + 1) * HEAD_DIM)
                blk = acc[:, sub]
                ms = jnp.mean(blk * blk, axis=-1, keepdims=True)
                o_ref[:, hcols] = (blk * lax.rsqrt(ms + RMS_EPS) * vec_ref[:, hcols]).astype(o_ref.dtype)


def _in_proj(x2d, g, w_all, layer, src_blocks, vec, *, kind, tm, tn, name):
    t, d = x2d.shape
    nj = len(src_blocks)
    kern = functools.partial(_in_proj_kernel, kind=kind, chunk=min(MXU_DIM, tn))
    return pl.pallas_call(
        kern,
        out_shape=jax.ShapeDtypeStruct((t, nj * tn), BF16),
        grid_spec=pltpu.PrefetchScalarGridSpec(
            num_scalar_prefetch=1,
            grid=(t // tm, nj),
            in_specs=[
                pl.BlockSpec((tm, d), lambda i, j, src: (i, 0)),
                pl.BlockSpec((1, d), lambda i, j, src: (0, 0)),
                pl.BlockSpec((None, d, tn), lambda i, j, src: (layer, 0, src[j])),
                pl.BlockSpec((1, tn), lambda i, j, src: (0, j)),
            ],
            out_specs=pl.BlockSpec((tm, tn), lambda i, j, src: (i, j)),
            scratch_shapes=[pltpu.VMEM((tm, d), BF16)],
        ),
        compiler_params=pltpu.CompilerParams(
            dimension_semantics=("parallel", "arbitrary"),
            vmem_limit_bytes=48 * MIB),
        name=name,
    )(jnp.asarray(src_blocks, jnp.int32), x2d, g, w_all, vec)


def _alibi_table(slopes, tq, s):
    r = jnp.arange(tq, dtype=jnp.int32)[:, None]
    j = jnp.arange(s, dtype=jnp.int32)[None, :]
    dist = (r - j + (s - tq)).astype(F32)
    return jnp.where(dist >= 0.0, (-LOG2E * slopes)[:, None, None] * dist[None], NEG_INF)


def _per_tile(qi, n_tiles, body):
    for c in range(n_tiles):
        pl.when(qi == c)(functools.partial(body, c))


def _head_cols(j, width=HEAD_DIM):
    return slice(j * width, (j + 1) * width)


def _moba_kernel(q_ref, qall_ref, k_ref, v_ref, bias_ref, o_ref, sel_ref,
                 *, hp, nb, blk, tq, topk, scale):
    qi = pl.program_id(2)
    s_len = nb * blk

    @pl.when(qi == 0)
    def _():
        nbp = -(-nb // BF16_SUBLANES) * BF16_SUBLANES
        row_id = lax.broadcasted_iota(jnp.int32, (nbp, HEAD_DIM), 0)
        blk_id = lax.broadcasted_iota(jnp.int32, (nbp, s_len), 0)
        own = lax.broadcasted_iota(jnp.int32, (nbp, s_len), 1) // blk
        blk_f = blk_id.astype(F32)
        for j in range(hp):
            hc = _head_cols(j)
            kmean = jnp.zeros((nbp, HEAD_DIM), F32)
            for n in range(nb):
                mean_n = jnp.sum(k_ref[n * blk:(n + 1) * blk, hc].astype(F32), axis=0, keepdims=True)
                kmean = jnp.where(row_id == n, mean_n * (1.0 / blk), kmean)
            km_hi, km_lo = _split_bf16(kmean)
            qall = qall_ref[:, hc]
            gate = _nt_dot(km_hi, qall) + _nt_dot(km_lo, qall)
            gate = jnp.where(blk_id < own, gate, NEG_INF)
            sel = jnp.where(blk_id == own, 1.0, 0.0)
            for _ in range(topk):
                gmax = jnp.max(gate, axis=0, keepdims=True)
                first = jnp.min(jnp.where(gate == gmax, blk_f, float(nbp)), axis=0, keepdims=True)
                pick = jnp.logical_and(blk_f == first, gmax > NEG_INF)
                sel = jnp.where(pick, 1.0, sel)
                gate = jnp.where(pick, NEG_INF, gate)
            mask = jnp.where(sel > 0.5, 0.0, NEG_INF)
            mask = jnp.concatenate([mask, jnp.full((LANES - nbp, s_len), NEG_INF, F32)], axis=0)
            sel_ref[j] = mask.T

    rows = pl.ds(pl.multiple_of(qi * tq, tq), tq)

    def tile(c):
        w = (c + 1) * tq
        for j in range(hp):
            hc = _head_cols(j)
            mask = sel_ref[j, rows, :]
            s = _nt_dot(q_ref[:, hc], k_ref[0:w, hc]) * scale + bias_ref[j, :, s_len - w:s_len]
            s = jnp.concatenate(
                [s[:, n * blk:(n + 1) * blk] + mask[:, n:n + 1] for n in range(w // blk)], axis=1)
            p = jnp.exp2(s - _row_max(s))
            o = _dot(p.astype(BF16), v_ref[0:w, hc]) / _row_sum(p)
            o_ref[:, hc] = o.astype(o_ref.dtype)

    _per_tile(qi, s_len // tq, tile)


def _moba(qk, rest, slopes, *, n_heads, hp, q_col, k_col, v_col):
    b, s, _ = qk.shape
    blk = MOBA_BLOCK
    tq = min(MOBA_TILE, s)
    nb = s // blk
    assert s % blk == 0 and nb <= LANES and tq % blk == 0 and s % tq == 0
    assert n_heads % hp == 0 and q_col % hp == 0 and k_col % hp == 0 and v_col % hp == 0
    bias = _alibi_table(slopes, tq, s)
    kern = functools.partial(_moba_kernel, hp=hp, nb=nb, blk=blk, tq=tq, topk=min(MOBA_TOPK, nb),
                             scale=LOG2E * HEAD_DIM ** -0.5)
    wd = hp * HEAD_DIM
    return pl.pallas_call(
        kern,
        out_shape=jax.ShapeDtypeStruct((b, s, n_heads * HEAD_DIM), BF16),
        grid=(b, n_heads // hp, s // tq),
        in_specs=[
            pl.BlockSpec((None, tq, wd), lambda bi, h, qi: (bi, qi, q_col // hp + h)),
            pl.BlockSpec((None, s, wd), lambda bi, h, qi: (bi, 0, q_col // hp + h)),
            pl.BlockSpec((None, s, wd), lambda bi, h, qi: (bi, 0, k_col // hp + h)),
            pl.BlockSpec((None, s, wd), lambda bi, h, qi: (bi, 0, v_col // hp + h)),
            pl.BlockSpec((hp, tq, s), lambda bi, h, qi: (h, 0, 0)),
        ],
        out_specs=pl.BlockSpec((None, tq, wd), lambda bi, h, qi: (bi, qi, h)),
        scratch_shapes=[pltpu.VMEM((hp, s, LANES), F32)],
        compiler_params=pltpu.CompilerParams(
            dimension_semantics=("parallel", "parallel", "arbitrary"),
            vmem_limit_bytes=48 * MIB),
        name="moba",
    )(qk, qk, qk, rest, bias)


def _diff_kernel(lam_ref, q0_ref, q1_ref, k0_ref, k1_ref, v_ref, bias_ref, gain_ref, o_ref,
                 *, nq, tq, scale, out_scale):
    qi = pl.program_id(2)
    lam = lam_ref[0]
    s_len = nq * tq

    def tile(c):
        w = (c + 1) * tq
        bias = bias_ref[:, s_len - w:s_len]
        v = v_ref[0:w, :]

        def attend(q_ref, k_ref):
            s = _nt_dot(q_ref[...], k_ref[0:w, :]) * scale + bias
            p = jnp.exp2(s - _row_max(s))
            return _dot(p.astype(BF16), v) / _row_sum(p)

        o = attend(q0_ref, k0_ref) - lam * attend(q1_ref, k1_ref)
        ms = jnp.mean(o * o, axis=-1, keepdims=True)
        o_ref[...] = (o * lax.rsqrt(ms + RMS_EPS) * gain_ref[...] * out_scale).astype(o_ref.dtype)

    _per_tile(qi, nq, tile)


def _diff_attention(qk, rest, slopes, lam, head_gain, *, n_heads, q_col, k_col, v_col2, out_scale):
    b, s, _ = qk.shape
    tq = min(ATT_TILE, s)
    nq = s // tq
    dv = 2 * HEAD_DIM
    bias = _alibi_table(slopes, tq, s)
    kern = functools.partial(_diff_kernel, nq=nq, tq=tq, scale=LOG2E * HEAD_DIM ** -0.5,
                             out_scale=out_scale)
    qspec = lambda m: pl.BlockSpec((None, tq, HEAD_DIM),
                                   lambda bi, h, qi, lm: (bi, qi, q_col + 2 * h + m))
    kspec = lambda m: pl.BlockSpec((None, s, HEAD_DIM),
                                   lambda bi, h, qi, lm: (bi, 0, k_col + 2 * h + m))
    return pl.pallas_call(
        kern,
        out_shape=jax.ShapeDtypeStruct((b, s, n_heads * dv), BF16),
        grid_spec=pltpu.PrefetchScalarGridSpec(
            num_scalar_prefetch=1,
            grid=(b, n_heads, nq),
            in_specs=[
                qspec(0), qspec(1), kspec(0), kspec(1),
                pl.BlockSpec((None, s, dv), lambda bi, h, qi, lm: (bi, 0, v_col2 + h)),
                pl.BlockSpec((None, tq, s), lambda bi, h, qi, lm: (h, 0, 0)),
                pl.BlockSpec((1, dv), lambda bi, h, qi, lm: (0, 0)),
            ],
            out_specs=pl.BlockSpec((None, tq, dv), lambda bi, h, qi, lm: (bi, qi, h)),
        ),
        compiler_params=pltpu.CompilerParams(
            dimension_semantics=("parallel", "parallel", "arbitrary"),
            vmem_limit_bytes=48 * MIB),
        name="diff_attn",
    )(lam, qk, qk, qk, qk, rest, bias, head_gain)


def _sb_kernel(q_ref, k_ref, v_ref, o_ref, *, hp, nq, tq, kb, scale):
    qi = pl.program_id(2)
    row = lax.broadcasted_iota(jnp.int32, (tq, kb), 0)
    col = lax.broadcasted_iota(jnp.int32, (tq, kb), 1)
    urow = lax.broadcasted_iota(jnp.int32, (kb, kb), 0)
    ucol = lax.broadcasted_iota(jnp.int32, (kb, kb), 1)
    upper = jnp.where(urow > ucol, 1.0, 0.0).astype(BF16)

    def tile(c):
        carry = [None] * hp
        out = [None] * hp
        first_diag = c * tq // kb
        for n in range((c + 1) * tq // kb - 1, -1, -1):
            ks = slice(n * kb, (n + 1) * kb)
            strict = (col + n * kb) < (row + c * tq)
            for j in range(hp):
                hc = _head_cols(j)
                nz = _nt_dot(q_ref[:, hc], k_ref[ks, hc]) * (-scale)
                log_keep = jnp.minimum(nz, 0.0) - jnp.log(1.0 + jnp.exp(-jnp.abs(nz)))
                if n >= first_diag:
                    log_keep = jnp.where(strict, log_keep, 0.0)
                after = _dot(log_keep.astype(BF16), upper)
                if carry[j] is not None:
                    after = after + carry[j]
                a = jnp.exp(log_keep - nz + after)
                if n >= first_diag:
                    a = jnp.where(strict, a, 0.0)
                pv = _dot(a.astype(BF16), v_ref[ks, hc])
                out[j] = pv if out[j] is None else out[j] + pv
                if n > 0:
                    rs = _row_sum(log_keep)
                    carry[j] = rs if carry[j] is None else carry[j] + rs
        for j in range(hp):
            o_ref[:, _head_cols(j)] = out[j].astype(o_ref.dtype)

    _per_tile(qi, nq, tile)


def _sb_attention(rest, *, n_heads, hp, q_col, k_col, v_col):
    b, s, _ = rest.shape
    tq = min(ATT_TILE, s)
    kb = ATT_BLOCK
    nq = s // tq
    assert n_heads % hp == 0 and q_col % hp == 0 and k_col % hp == 0 and v_col % hp == 0
    assert s % tq == 0 and tq % kb == 0
    kern = functools.partial(_sb_kernel, hp=hp, nq=nq, tq=tq, kb=kb, scale=HEAD_DIM ** -0.5)
    wd = hp * HEAD_DIM
    return pl.pallas_call(
        kern,
        out_shape=jax.ShapeDtypeStruct((b, s, n_heads * HEAD_DIM), BF16),
        grid=(b, n_heads // hp, nq),
        in_specs=[
            pl.BlockSpec((None, tq, wd), lambda bi, h, qi: (bi, qi, q_col // hp + h)),
            pl.BlockSpec((None, s, wd), lambda bi, h, qi: (bi, 0, k_col // hp + h)),
            pl.BlockSpec((None, s, wd), lambda bi, h, qi: (bi, 0, v_col // hp + h)),
        ],
        out_specs=pl.BlockSpec((None, tq, wd), lambda bi, h, qi: (bi, qi, h)),
        compiler_params=pltpu.CompilerParams(
            dimension_semantics=("parallel", "parallel", "arbitrary"),
            vmem_limit_bytes=48 * MIB),
        name="stick_breaking",
    )(rest, rest, rest)


def _route_tile(lg):
    lane = lax.broadcasted_iota(jnp.int32, lg.shape, 1).astype(F32)
    first_of = lambda mask: jnp.min(jnp.where(mask, lane, float(LANES)), axis=-1, keepdims=True)

    is_group = lane < float(N_GROUPS)
    gl = jnp.where(is_group, lg, NEG_INF)
    ge = jnp.where(is_group, jnp.exp(gl - _row_max(gl)), 0.0)
    g_prob = ge / _row_sum(ge)
    g_top = _row_max(g_prob)
    g_idx = first_of(g_prob == g_top)

    e_lo = float(N_GROUPS) + float(EXPERTS_PER_GROUP) * g_idx
    in_group = jnp.logical_and(lane >= e_lo, lane < e_lo + float(EXPERTS_PER_GROUP))
    el = jnp.where(in_group, lg, NEG_INF)
    ee = jnp.where(in_group, jnp.exp(el - _row_max(el)), 0.0)
    e_prob = jnp.where(in_group, ee / _row_sum(ee), -1.0)
    top1 = _row_max(e_prob)
    idx1 = first_of(e_prob == top1)
    e_rest = jnp.where(lane == idx1, -1.0, e_prob)
    top2 = _row_max(e_rest)
    idx2 = first_of(e_rest == top2)
    denom = top1 + top2
    w1 = g_top * top1 / denom
    w2 = g_top * top2 / denom
    out = jnp.where(lane == 0.0, w1, 0.0)
    out = jnp.where(lane == 1.0, w2, out)
    out = jnp.where(lane == 2.0, idx1 - float(N_GROUPS), out)
    out = jnp.where(lane == 3.0, idx2 - float(N_GROUPS), out)
    return out


def _merge_kernel(oa_ref, ob_ref, oc_ref, ga_ref, gb_ref, gc_ref, x_ref, wb_ref, wo_ref, fg_ref,
                  wr_hi_ref, wr_lo_ref, xo_ref, h_ref, route_ref):
    mixed = ga_ref[...].astype(F32) * _dot(oa_ref[...], wb_ref[0])
    mixed += gb_ref[...].astype(F32) * _dot(ob_ref[...], wb_ref[1])
    mixed += gc_ref[...].astype(F32) * _dot(oc_ref[...], wb_ref[2])
    xn = x_ref[...] + _dot(mixed.astype(BF16), wo_ref[...])
    xo_ref[...] = xn
    ms = jnp.mean(xn * xn, axis=-1, keepdims=True)
    hn = xn * lax.rsqrt(ms + RMS_EPS) * fg_ref[...]
    h_hi, h_lo = _split_bf16(hn)
    h_ref[...] = h_hi
    logits = (_dot(h_hi, wr_hi_ref[...]) + _dot(h_hi, wr_lo_ref[...]) + _dot(h_lo, wr_hi_ref[...]))
    route_ref[...] = _route_tile(logits)


def _merge(oa, ob, oc, gates, x2d, wb, wo, fg, wr_hi, wr_lo, *, tm):
    t, d = x2d.shape
    mw = oa.shape[1]
    nr = wr_hi.shape[1]
    const = lambda *shape: pl.BlockSpec(shape, lambda i: (0,) * len(shape),
                                        pipeline_mode=pl.Buffered(1))
    gspec = lambda g: pl.BlockSpec((tm, d), lambda i: (i, g))
    return pl.pallas_call(
        _merge_kernel,
        out_shape=(jax.ShapeDtypeStruct((t, d), F32),
                   jax.ShapeDtypeStruct((t, d), BF16),
                   jax.ShapeDtypeStruct((t, nr), F32)),
        grid=(t // tm,),
        in_specs=[
            pl.BlockSpec((tm, mw), lambda i: (i, 0)),
            pl.BlockSpec((tm, mw), lambda i: (i, 0)),
            pl.BlockSpec((tm, mw), lambda i: (i, 0)),
            gspec(0), gspec(1), gspec(2),
            pl.BlockSpec((tm, d), lambda i: (i, 0)),
            const(N_BRANCH, mw, d),
            const(d, d),
            const(1, d),
            const(d, nr),
            const(d, nr),
        ],
        out_specs=(pl.BlockSpec((tm, d), lambda i: (i, 0)),
                   pl.BlockSpec((tm, d), lambda i: (i, 0)),
                   pl.BlockSpec((tm, nr), lambda i: (i, 0))),
        compiler_params=pltpu.CompilerParams(
            dimension_semantics=("parallel",),
            vmem_limit_bytes=56 * MIB),
        name="merge_out",
    )(oa, ob, oc, gates, gates, gates, x2d, wb, wo, fg, wr_hi, wr_lo)


def _moe_kernel(be_ref, nused_ref, xs_ref, sw_ref, wg_ref, wu_ref, wd_ref, ys_ref,
                wgb_ref, wub_ref, wdb_ref):
    i = pl.program_id(0)
    live = i < nused_ref[0]
    new_expert = jnp.logical_or(i == 0, be_ref[i] != be_ref[jnp.maximum(i - 1, 0)])

    @pl.when(jnp.logical_and(live, new_expert))
    def _():
        wgb_ref[...] = wg_ref[...].astype(BF16)
        wub_ref[...] = wu_ref[...].astype(BF16)
        wdb_ref[...] = wd_ref[...].astype(BF16)

    @pl.when(live)
    def _():
        x = xs_ref[...]
        g = _dot(x, wgb_ref[...])
        u = _dot(x, wub_ref[...])
        act = g * (1.0 / (1.0 + jnp.exp(-g))) * u
        ys_ref[...] = (_dot(act.astype(BF16), wdb_ref[...]) * sw_ref[...]).astype(ys_ref.dtype)

    @pl.when(jnp.logical_not(live))
    def _():
        ys_ref[...] = jnp.zeros_like(ys_ref)


def _moe_experts(xs, slot_w, block_expert, n_used, wg, wu, wd, layer):
    n_slots, d = xs.shape
    ff = wg.shape[3]
    n_blocks = n_slots // MOE_BLOCK
    return pl.pallas_call(
        _moe_kernel,
        out_shape=jax.ShapeDtypeStruct((n_slots, d), BF16),
        grid_spec=pltpu.PrefetchScalarGridSpec(
            num_scalar_prefetch=2,
            grid=(n_blocks,),
            in_specs=[
                pl.BlockSpec((MOE_BLOCK, d), lambda i, be, nu: (i, 0)),
                pl.BlockSpec((MOE_BLOCK, 1), lambda i, be, nu: (i, 0)),
                pl.BlockSpec((None, None, d, ff), lambda i, be, nu: (layer, be[i], 0, 0)),
                pl.BlockSpec((None, None, d, ff), lambda i, be, nu: (layer, be[i], 0, 0)),
                pl.BlockSpec((None, None, ff, d), lambda i, be, nu: (layer, be[i], 0, 0)),
            ],
            out_specs=pl.BlockSpec((MOE_BLOCK, d), lambda i, be, nu: (i, 0)),
            scratch_shapes=[pltpu.VMEM((d, ff), BF16), pltpu.VMEM((d, ff), BF16),
                            pltpu.VMEM((ff, d), BF16)],
        ),
        compiler_params=pltpu.CompilerParams(
            dimension_semantics=("arbitrary",),
            vmem_limit_bytes=52 * MIB),
        name="moe_experts",
    )(block_expert, n_used, xs, slot_w, wg, wu, wd)


def _dispatch_plan(expert, weights):
    t = expert.shape[0]
    n_assign = t * EXPERT_TOPK
    flat_e = expert.reshape(-1)
    iota = jnp.arange(n_assign, dtype=jnp.int32)
    _, order, w_sorted = lax.sort((flat_e, iota, weights.reshape(-1)), num_keys=1, is_stable=True)
    _, rank = lax.sort((order, iota), num_keys=1, is_stable=True)
    ids = jnp.arange(N_EXPERTS, dtype=jnp.int32)
    onehot = flat_e[:, None] == ids[None, :]
    counts = jnp.sum(onehot.astype(jnp.int32), axis=0)
    padded = ((counts + MOE_BLOCK - 1) // MOE_BLOCK) * MOE_BLOCK
    pend = jnp.cumsum(padded)
    pstart = pend - padded
    cstart = jnp.cumsum(counts) - counts
    n_blocks = -(-n_assign // MOE_BLOCK) + N_EXPERTS
    block_start = jnp.arange(n_blocks, dtype=jnp.int32) * MOE_BLOCK
    block_expert = jnp.minimum(jnp.sum((pend[None, :] <= block_start[:, None]).astype(jnp.int32), axis=1),
                               N_EXPERTS - 1)
    off = (block_start - pstart[block_expert])[:, None] + jnp.arange(MOE_BLOCK, dtype=jnp.int32)[None, :]
    pos = jnp.clip(cstart[block_expert][:, None] + off, 0, n_assign - 1).reshape(-1)
    slot_token = order[pos] // EXPERT_TOPK
    slot_w = w_sorted[pos][:, None]
    shift = jnp.sum(jnp.where(onehot, (pstart - cstart)[None, :], 0), axis=1)
    assign_slot = (rank + shift).reshape(t, EXPERT_TOPK)
    n_used = (pend[-1] // MOE_BLOCK).astype(jnp.int32).reshape(1)
    return slot_token, slot_w, assign_slot, block_expert.astype(jnp.int32), n_used


def _alibi_slopes(n_heads):
    return jnp.exp2(-8.0 * jnp.arange(1, n_heads + 1, dtype=F32) / n_heads)


def kernel(x, attn_norm_g, w_in, gate_bias, moba_q_gain, moba_k_gain, diff_q_gain, diff_k_gain,
           diff_lambda_q1, diff_lambda_k1, diff_lambda_q2, diff_lambda_k2, diff_head_gain,
           w_branch, w_out, ffn_norm_g, w_router_group, w_router_expert,
           w_expert_gate, w_expert_up, w_expert_down):
    b, s, d = x.shape
    depth = w_in.shape[0]
    mw = d // 2
    heads = mw // HEAD_DIM
    diff_heads = mw // (2 * HEAD_DIM)
    hb = mw // HEAD_DIM
    t = b * s
    tn = min(1024, mw)
    tm = min(1024, t)
    assert mw % tn == 0 and d % tn == 0 and t % tm == 0

    def seg_blocks(mixer, part):
        first = (mixer * 3 + part) * mw // tn
        return list(range(first, first + mw // tn))

    gate_blocks = list(range(9 * mw // tn, (9 * mw + N_BRANCH * d) // tn))
    qk_blocks = seg_blocks(0, 0) + seg_blocks(0, 1) + seg_blocks(1, 0) + seg_blocks(1, 1)
    rest_blocks = (seg_blocks(0, 2) + seg_blocks(1, 2) + seg_blocks(2, 0) + seg_blocks(2, 1)
                   + seg_blocks(2, 2))
    w_in_bf16 = w_in.astype(BF16)

    slopes_moba = _alibi_slopes(heads)
    slopes_diff = _alibi_slopes(diff_heads)
    w_branch_bf16 = w_branch.astype(BF16)
    w_out_bf16 = w_out.astype(BF16)

    def layer(xc, l):
        tc = xc.shape[0]
        bc = tc // s
        qk_gain = jnp.concatenate([jnp.tile(moba_q_gain[l], hb), jnp.tile(moba_k_gain[l], hb),
                                   jnp.tile(diff_q_gain[l], hb), jnp.tile(diff_k_gain[l], hb)])[None, :]
        g_attn = attn_norm_g[l][None, :]
        gates = _in_proj(xc, g_attn, w_in_bf16, l, gate_blocks, gate_bias[l].reshape(1, -1),
                         kind=KIND_GATE, tm=tm, tn=tn, name="in_proj_gate")
        qk = _in_proj(xc, g_attn, w_in_bf16, l, qk_blocks, qk_gain, kind=KIND_NORM, tm=tm, tn=tn,
                      name="in_proj_qk").reshape(bc, s, -1)
        rest = _in_proj(xc, g_attn, w_in_bf16, l, rest_blocks, jnp.ones((1, 5 * mw), F32),
                        kind=KIND_PLAIN, tm=tm, tn=tn, name="in_proj_rest").reshape(bc, s, -1)

        hp = min(HEADS_PER_STEP, heads)
        oa = _moba(qk, rest, slopes_moba, n_heads=heads, hp=hp, q_col=0, k_col=hb, v_col=0)

        lam_init = 0.8 - 0.6 * math.exp(-0.3 * l)
        lam = (jnp.exp(jnp.sum(diff_lambda_q1[l] * diff_lambda_k1[l]))
               - jnp.exp(jnp.sum(diff_lambda_q2[l] * diff_lambda_k2[l])) + lam_init)
        ob = _diff_attention(qk, rest, slopes_diff, lam.reshape(1).astype(F32), diff_head_gain[l][None, :],
                             n_heads=diff_heads, q_col=2 * hb, k_col=3 * hb, v_col2=hb // 2,
                             out_scale=1.0 - lam_init)

        oc = _sb_attention(rest, n_heads=heads, hp=hp, q_col=2 * hb, k_col=3 * hb,
                           v_col=4 * hb)

        w_router = jnp.concatenate([w_router_group[l], w_router_expert[l]], axis=1)
        w_router = jnp.pad(w_router, ((0, 0), (0, LANES - w_router.shape[1])))
        wr_hi, wr_lo = _split_bf16(w_router)
        x_mid, h2d, route = _merge(
            oa.reshape(tc, mw), ob.reshape(tc, mw), oc.reshape(tc, mw), gates, xc,
            w_branch_bf16[l], w_out_bf16[l], ffn_norm_g[l][None, :], wr_hi, wr_lo, tm=256)

        weights = route[:, 0:EXPERT_TOPK]
        expert = route[:, EXPERT_TOPK:2 * EXPERT_TOPK].astype(jnp.int32)
        slot_token, slot_w, assign_slot, block_expert, n_used = _dispatch_plan(expert, weights)
        ys = _moe_experts(h2d[slot_token], slot_w, block_expert, n_used,
                          w_expert_gate, w_expert_up, w_expert_down, l)
        y = ys[assign_slot[:, 0]].astype(F32) + ys[assign_slot[:, 1]].astype(F32)
        return x_mid + y

    n_chunks = BATCH_CHUNKS if b % BATCH_CHUNKS == 0 else 1
    tc = t // n_chunks
    tm = min(tm, tc)
    x2d = x.reshape(t, d)
    chunks = [x2d[c * tc:(c + 1) * tc] for c in range(n_chunks)]
    for l in range(depth):
        chunks = [layer(xc, l) for xc in chunks]
    return jnp.concatenate(chunks, axis=0).reshape(b, s, d)
```

```python
import functools
import math

import jax
import jax.numpy as jnp
from jax import lax
from jax.experimental import pallas as pl
from jax.experimental.pallas import tpu as pltpu

F32 = jnp.float32
BF16 = jnp.bfloat16

HEAD_DIM = 128
N_BRANCH = 3
MOBA_BLOCK = 256
MOBA_TOPK = 3
N_GROUPS = 4
EXPERTS_PER_GROUP = 8
N_EXPERTS = N_GROUPS * EXPERTS_PER_GROUP
EXPERT_TOPK = 2
MOE_BLOCK = 256
RMS_EPS = 1e-6
ATT_BLOCK = 256
ATT_TILE = 512
SB_TILE = 512
MOBA_TILE = 256
BATCH_CHUNKS = 1
HEADS_PER_STEP = 4
LANES = 128
BF16_SUBLANES = 16
MXU_DIM = 256
NEG_INF = float("-inf")
LOG2E = 1.4426950408889634
MIB = 1024 * 1024

KIND_PLAIN, KIND_NORM, KIND_GATE = 0, 1, 2


def _nt_dot(a, b):
    return lax.dot_general(a, b, (((1,), (1,)), ((), ())), preferred_element_type=F32)


def _dot(a, b):
    return jnp.dot(a, b, preferred_element_type=F32)


def _split_bf16(x):
    hi = x.astype(BF16)
    lo = (x - hi.astype(F32)).astype(BF16)
    return hi, lo


def _row_max(x):
    return jnp.max(x, axis=-1, keepdims=True)


def _row_sum(x):
    return jnp.sum(x, axis=-1, keepdims=True)


def _in_proj_kernel(src_ref, x_ref, g_ref, w_ref, vec_ref, o_ref, xn_ref, *, kind, chunk):
    del src_ref
    @pl.when(pl.program_id(1) == 0)
    def _():
        x = x_ref[...]
        ms = jnp.mean(x * x, axis=-1, keepdims=True)
        xn_ref[...] = (x * lax.rsqrt(ms + RMS_EPS) * g_ref[...]).astype(BF16)

    xn = xn_ref[...]
    for c in range(o_ref.shape[1] // chunk):
        cols = slice(c * chunk, (c + 1) * chunk)
        acc = _dot(xn, w_ref[:, cols])
        if kind == KIND_PLAIN:
            o_ref[:, cols] = acc.astype(o_ref.dtype)
        elif kind == KIND_GATE:
            z = acc + vec_ref[:, cols]
            o_ref[:, cols] = (0.5 * jnp.tanh(0.5 * z) + 0.5).astype(o_ref.dtype)
        else:
            for hd in range(chunk // HEAD_DIM):
                sub = slice(hd * HEAD_DIM, (hd + 1) * HEAD_DIM)
                hcols = slice(c * chunk + hd * HEAD_DIM, c * chunk + (hd + 1) * HEAD_DIM)
                blk = acc[:, sub]
                ms = jnp.mean(blk * blk, axis=-1, keepdims=True)
                o_ref[:, hcols] = (blk * lax.rsqrt(ms + RMS_EPS) * vec_ref[:, hcols]).astype(o_ref.dtype)


def _in_proj(x2d, g, w_all, layer, src_blocks, vec, *, kind, tm, tn, name):
    t, d = x2d.shape
    nj = len(src_blocks)
    kern = functools.partial(_in_proj_kernel, kind=kind, chunk=min(MXU_DIM, tn))
    return pl.pallas_call(
        kern,
        out_shape=jax.ShapeDtypeStruct((t, nj * tn), BF16),
        grid_spec=pltpu.PrefetchScalarGridSpec(
            num_scalar_prefetch=1,
            grid=(t // tm, nj),
            in_specs=[
                pl.BlockSpec((tm, d), lambda i, j, src: (i, 0)),
                pl.BlockSpec((1, d), lambda i, j, src: (0, 0)),
                pl.BlockSpec((None, d, tn), lambda i, j, src: (layer, 0, src[j])),
                pl.BlockSpec((1, tn), lambda i, j, src: (0, j)),
            ],
            out_specs=pl.BlockSpec((tm, tn), lambda i, j, src: (i, j)),
            scratch_shapes=[pltpu.VMEM((tm, d), BF16)],
        ),
        compiler_params=pltpu.CompilerParams(
            dimension_semantics=("parallel", "arbitrary"),
            vmem_limit_bytes=48 * MIB),
        name=name,
    )(jnp.asarray(src_blocks, jnp.int32), x2d, g, w_all, vec)


def _alibi_table(slopes, tq, s):
    r = jnp.arange(tq, dtype=jnp.int32)[:, None]
    j = jnp.arange(s, dtype=jnp.int32)[None, :]
    dist = (r - j + (s - tq)).astype(F32)
    return jnp.where(dist >= 0.0, (-LOG2E * slopes)[:, None, None] * dist[None], NEG_INF)


def _per_tile(qi, n_tiles, body):
    for c in range(n_tiles):
        pl.when(qi == c)(functools.partial(body, c))


def _head_cols(j, width=HEAD_DIM):
    return slice(j * width, (j + 1) * width)


def _moba_kernel(q_ref, qall_ref, k_ref, v_ref, bias_ref, o_ref, sel_ref,
                 *, hp, nb, blk, tq, topk, scale):
    qi = pl.program_id(2)
    s_len = nb * blk

    @pl.when(qi == 0)
    def _():
        nbp = -(-nb // BF16_SUBLANES) * BF16_SUBLANES
        row_id = lax.broadcasted_iota(jnp.int32, (nbp, HEAD_DIM), 0)
        blk_id = lax.broadcasted_iota(jnp.int32, (nbp, s_len), 0)
        own = lax.broadcasted_iota(jnp.int32, (nbp, s_len), 1) // blk
        blk_f = blk_id.astype(F32)
        for j in range(hp):
            hc = _head_cols(j)
            kmean = jnp.zeros((nbp, HEAD_DIM), F32)
            for n in range(nb):
                mean_n = jnp.sum(k_ref[n * blk:(n + 1) * blk, hc].astype(F32), axis=0, keepdims=True)
                kmean = jnp.where(row_id == n, mean_n * (1.0 / blk), kmean)
            km_hi, km_lo = _split_bf16(kmean)
            qall = qall_ref[:, hc]
            gate = _nt_dot(km_hi, qall) + _nt_dot(km_lo, qall)
            gate = jnp.where(blk_id < own, gate, NEG_INF)
            sel = jnp.where(blk_id == own, 1.0, 0.0)
            for _ in range(topk):
                gmax = jnp.max(gate, axis=0, keepdims=True)
                first = jnp.min(jnp.where(gate == gmax, blk_f, float(nbp)), axis=0, keepdims=True)
                pick = jnp.logical_and(blk_f == first, gmax > NEG_INF)
                sel = jnp.where(pick, 1.0, sel)
                gate = jnp.where(pick, NEG_INF, gate)
            mask = jnp.where(sel > 0.5, 0.0, NEG_INF)
            mask = jnp.concatenate([mask, jnp.full((LANES - nbp, s_len), NEG_INF, F32)], axis=0)
            sel_ref[j] = mask.T

    rows = pl.ds(pl.multiple_of(qi * tq, tq), tq)

    def tile(c):
        w = (c + 1) * tq
        for j in range(hp):
            hc = _head_cols(j)
            mask = sel_ref[j, rows, :]
            s = _nt_dot(q_ref[:, hc], k_ref[0:w, hc]) * scale + bias_ref[j, :, s_len - w:s_len]
            s = jnp.concatenate(
                [s[:, n * blk:(n + 1) * blk] + mask[:, n:n + 1] for n in range(w // blk)], axis=1)
            p = jnp.exp2(s - _row_max(s))
            o = _dot(p.astype(BF16), v_ref[0:w, hc]) / _row_sum(p)
            o_ref[:, hc] = o.astype(o_ref.dtype)

    _per_tile(qi, s_len // tq, tile)


def _moba(qk, rest, slopes, *, n_heads, hp, q_col, k_col, v_col):
    b, s, _ = qk.shape
    blk = MOBA_BLOCK
    tq = min(MOBA_TILE, s)
    nb = s // blk
    assert s % blk == 0 and nb <= LANES and tq % blk == 0 and s % tq == 0
    assert n_heads % hp == 0 and q_col % hp == 0 and k_col % hp == 0 and v_col % hp == 0
    bias = _alibi_table(slopes, tq, s)
    kern = functools.partial(_moba_kernel, hp=hp, nb=nb, blk=blk, tq=tq, topk=min(MOBA_TOPK, nb),
                             scale=LOG2E * HEAD_DIM ** -0.5)
    wd = hp * HEAD_DIM
    return pl.pallas_call(
        kern,
        out_shape=jax.ShapeDtypeStruct((b, s, n_heads * HEAD_DIM), BF16),
        grid=(b, n_heads // hp, s // tq),
        in_specs=[
            pl.BlockSpec((None, tq, wd), lambda bi, h, qi: (bi, qi, q_col // hp + h)),
            pl.BlockSpec((None, s, wd), lambda bi, h, qi: (bi, 0, q_col // hp + h)),
            pl.BlockSpec((None, s, wd), lambda bi, h, qi: (bi, 0, k_col // hp + h)),
            pl.BlockSpec((None, s, wd), lambda bi, h, qi: (bi, 0, v_col // hp + h)),
            pl.BlockSpec((hp, tq, s), lambda bi, h, qi: (h, 0, 0)),
        ],
        out_specs=pl.BlockSpec((None, tq, wd), lambda bi, h, qi: (bi, qi, h)),
        scratch_shapes=[pltpu.VMEM((hp, s, LANES), F32)],
        compiler_params=pltpu.CompilerParams(
            dimension_semantics=("parallel", "parallel", "arbitrary"),
            vmem_limit_bytes=48 * MIB),
        name="moba",
    )(qk, qk, qk, rest, bias)


def _diff_kernel(lam_ref, q0_ref, q1_ref, k0_ref, k1_ref, v_ref, bias_ref, gain_ref, o_ref,
                 *, nq, tq, scale, out_scale):
    qi = pl.program_id(2)
    lam = lam_ref[0]
    s_len = nq * tq

    def tile(c):
        w = (c + 1) * tq
        bias = bias_ref[:, s_len - w:s_len]
        v = v_ref[0:w, :]

        def attend(q_ref, k_ref):
            s = _nt_dot(q_ref[...], k_ref[0:w, :]) * scale + bias
            p = jnp.exp2(s - _row_max(s))
            return _dot(p.astype(BF16), v) / _row_sum(p)

        o = attend(q0_ref, k0_ref) - lam * attend(q1_ref, k1_ref)
        ms = jnp.mean(o * o, axis=-1, keepdims=True)
        o_ref[...] = (o * lax.rsqrt(ms + RMS_EPS) * gain_ref[...] * out_scale).astype(o_ref.dtype)

    _per_tile(qi, nq, tile)


def _diff_attention(qk, rest, slopes, lam, head_gain, *, n_heads, q_col, k_col, v_col2, out_scale):
    b, s, _ = qk.shape
    tq = min(ATT_TILE, s)
    nq = s // tq
    dv = 2 * HEAD_DIM
    bias = _alibi_table(slopes, tq, s)
    kern = functools.partial(_diff_kernel, nq=nq, tq=tq, scale=LOG2E * HEAD_DIM ** -0.5,
                             out_scale=out_scale)
    qspec = lambda m: pl.BlockSpec((None, tq, HEAD_DIM),
                                   lambda bi, h, qi, lm: (bi, qi, q_col + 2 * h + m))
    kspec = lambda m: pl.BlockSpec((None, s, HEAD_DIM),
                                   lambda bi, h, qi, lm: (bi, 0, k_col + 2 * h + m))
    return pl.pallas_call(
        kern,
        out_shape=jax.ShapeDtypeStruct((b, s, n_heads * dv), BF16),
        grid_spec=pltpu.PrefetchScalarGridSpec(
            num_scalar_prefetch=1,
            grid=(b, n_heads, nq),
            in_specs=[
                qspec(0), qspec(1), kspec(0), kspec(1),
                pl.BlockSpec((None, s, dv), lambda bi, h, qi, lm: (bi, 0, v_col2 + h)),
                pl.BlockSpec((None, tq, s), lambda bi, h, qi, lm: (h, 0, 0)),
                pl.BlockSpec((1, dv), lambda bi, h, qi, lm: (0, 0)),
            ],
            out_specs=pl.BlockSpec((None, tq, dv), lambda bi, h, qi, lm: (bi, qi, h)),
        ),
        compiler_params=pltpu.CompilerParams(
            dimension_semantics=("parallel", "parallel", "arbitrary"),
            vmem_limit_bytes=48 * MIB),
        name="diff_attn",
    )(lam, qk, qk, qk, qk, rest, bias, head_gain)


def _sb_kernel(q_ref, k_ref, v_ref, o_ref, *, hp, nq, tq, kb, scale):
    qi = pl.program_id(2)
    row = lax.broadcasted_iota(jnp.int32, (tq, kb), 0)
    col = lax.broadcasted_iota(jnp.int32, (tq, kb), 1)
    urow = lax.broadcasted_iota(jnp.int32, (kb, kb), 0)
    ucol = lax.broadcasted_iota(jnp.int32, (kb, kb), 1)
    upper = jnp.where(urow > ucol, 1.0, 0.0).astype(BF16)

    def tile(c):
        carry = [None] * hp
        out = [None] * hp
        first_diag = c * tq // kb
        for n in range((c + 1) * tq // kb - 1, -1, -1):
            ks = slice(n * kb, (n + 1) * kb)
            strict = (col + n * kb) < (row + c * tq)
            for j in range(hp):
                hc = _head_cols(j)
                nz = _nt_dot(q_ref[:, hc], k_ref[ks, hc]) * (-scale)
                log_keep = jnp.minimum(nz, 0.0) - jnp.log(1.0 + jnp.exp(-jnp.abs(nz)))
                if n >= first_diag:
                    log_keep = jnp.where(strict, log_keep, 0.0)
                after = _dot(log_keep.astype(BF16), upper)
                if carry[j] is not None:
                    after = after + carry[j]
                a = jnp.exp(log_keep - nz + after)
                if n >= first_diag:
                    a = jnp.where(strict, a, 0.0)
                pv = _dot(a.astype(BF16), v_ref[ks, hc])
                out[j] = pv if out[j] is None else out[j] + pv
                if n > 0:
                    rs = _row_sum(log_keep)
                    carry[j] = rs if carry[j] is None else carry[j] + rs
        for j in range(hp):
            o_ref[:, _head_cols(j)] = out[j].astype(o_ref.dtype)

    _per_tile(qi, nq, tile)


def _sb_attention(rest, *, n_heads, hp, q_col, k_col, v_col):
    b, s, _ = rest.shape
    tq = min(SB_TILE, s)
    kb = ATT_BLOCK
    nq = s // tq
    assert n_heads % hp == 0 and q_col % hp == 0 and k_col % hp == 0 and v_col % hp == 0
    assert s % tq == 0 and tq % kb == 0
    kern = functools.partial(_sb_kernel, hp=hp, nq=nq, tq=tq, kb=kb, scale=HEAD_DIM ** -0.5)
    wd = hp * HEAD_DIM
    return pl.pallas_call(
        kern,
        out_shape=jax.ShapeDtypeStruct((b, s, n_heads * HEAD_DIM), BF16),
        grid=(b, n_heads // hp, nq),
        in_specs=[
            pl.BlockSpec((None, tq, wd), lambda bi, h, qi: (bi, qi, q_col // hp + h)),
            pl.BlockSpec((None, s, wd), lambda bi, h, qi: (bi, 0, k_col // hp + h)),
            pl.BlockSpec((None, s, wd), lambda bi, h, qi: (bi, 0, v_col // hp + h)),
        ],
        out_specs=pl.BlockSpec((None, tq, wd), lambda bi, h, qi: (bi, qi, h)),
        compiler_params=pltpu.CompilerParams(
            dimension_semantics=("parallel", "parallel", "arbitrary"),
            vmem_limit_bytes=48 * MIB),
        name="stick_breaking",
    )(rest, rest, rest)


def _route_tile(lg):
    lane = lax.broadcasted_iota(jnp.int32, lg.shape, 1).astype(F32)
    first_of = lambda mask: jnp.min(jnp.where(mask, lane, float(LANES)), axis=-1, keepdims=True)

    is_group = lane < float(N_GROUPS)
    gl = jnp.where(is_group, lg, NEG_INF)
    ge = jnp.where(is_group, jnp.exp(gl - _row_max(gl)), 0.0)
    g_prob = ge / _row_sum(ge)
    g_top = _row_max(g_prob)
    g_idx = first_of(g_prob == g_top)

    e_lo = float(N_GROUPS) + float(EXPERTS_PER_GROUP) * g_idx
    in_group = jnp.logical_and(lane >= e_lo, lane < e_lo + float(EXPERTS_PER_GROUP))
    el = jnp.where(in_group, lg, NEG_INF)
    ee = jnp.where(in_group, jnp.exp(el - _row_max(el)), 0.0)
    e_prob = jnp.where(in_group, ee / _row_sum(ee), -1.0)
    top1 = _row_max(e_prob)
    idx1 = first_of(e_prob == top1)
    e_rest = jnp.where(lane == idx1, -1.0, e_prob)
    top2 = _row_max(e_rest)
    idx2 = first_of(e_rest == top2)
    denom = top1 + top2
    w1 = g_top * top1 / denom
    w2 = g_top * top2 / denom
    out = jnp.where(lane == 0.0, w1, 0.0)
    out = jnp.where(lane == 1.0, w2, out)
    out = jnp.where(lane == 2.0, idx1 - float(N_GROUPS), out)
    out = jnp.where(lane == 3.0, idx2 - float(N_GROUPS), out)
    return out


def _merge_kernel(oa_ref, ob_ref, oc_ref, ga_ref, gb_ref, gc_ref, x_ref, wb_ref, wo_ref, fg_ref,
                  wr_hi_ref, wr_lo_ref, xo_ref, h_ref, route_ref):
    mixed = ga_ref[...].astype(F32) * _dot(oa_ref[...], wb_ref[0])
    mixed += gb_ref[...].astype(F32) * _dot(ob_ref[...], wb_ref[1])
    mixed += gc_ref[...].astype(F32) * _dot(oc_ref[...], wb_ref[2])
    xn = x_ref[...] + _dot(mixed.astype(BF16), wo_ref[...])
    xo_ref[...] = xn
    ms = jnp.mean(xn * xn, axis=-1, keepdims=True)
    hn = xn * lax.rsqrt(ms + RMS_EPS) * fg_ref[...]
    h_hi, h_lo = _split_bf16(hn)
    h_ref[...] = h_hi
    logits = (_dot(h_hi, wr_hi_ref[...]) + _dot(h_hi, wr_lo_ref[...]) + _dot(h_lo, wr_hi_ref[...]))
    route_ref[...] = _route_tile(logits)


def _merge(oa, ob, oc, gates, x2d, wb, wo, fg, wr_hi, wr_lo, *, tm):
    t, d = x2d.shape
    mw = oa.shape[1]
    nr = wr_hi.shape[1]
    const = lambda *shape: pl.BlockSpec(shape, lambda i: (0,) * len(shape),
                                        pipeline_mode=pl.Buffered(1))
    gspec = lambda g: pl.BlockSpec((tm, d), lambda i: (i, g))
    return pl.pallas_call(
        _merge_kernel,
        out_shape=(jax.ShapeDtypeStruct((t, d), F32),
                   jax.ShapeDtypeStruct((t, d), BF16),
                   jax.ShapeDtypeStruct((t, nr), F32)),
        grid=(t // tm,),
        in_specs=[
            pl.BlockSpec((tm, mw), lambda i: (i, 0)),
            pl.BlockSpec((tm, mw), lambda i: (i, 0)),
            pl.BlockSpec((tm, mw), lambda i: (i, 0)),
            gspec(0), gspec(1), gspec(2),
            pl.BlockSpec((tm, d), lambda i: (i, 0)),
            const(N_BRANCH, mw, d),
            const(d, d),
            const(1, d),
            const(d, nr),
            const(d, nr),
        ],
        out_specs=(pl.BlockSpec((tm, d), lambda i: (i, 0)),
                   pl.BlockSpec((tm, d), lambda i: (i, 0)),
                   pl.BlockSpec((tm, nr), lambda i: (i, 0))),
        compiler_params=pltpu.CompilerParams(
            dimension_semantics=("parallel",),
            vmem_limit_bytes=56 * MIB),
        name="merge_out",
    )(oa, ob, oc, gates, gates, gates, x2d, wb, wo, fg, wr_hi, wr_lo)


def _moe_kernel_aliased(be_ref, nused_ref, xs_ref, sw_ref, wg_ref, wu_ref, wd_ref, prev_ref, *refs):
    del prev_ref
    _moe_kernel(be_ref, nused_ref, xs_ref, sw_ref, wg_ref, wu_ref, wd_ref, *refs)


def _moe_kernel(be_ref, nused_ref, xs_ref, sw_ref, wg_ref, wu_ref, wd_ref, ys_ref,
                wgb_ref, wub_ref, wdb_ref):
    i = pl.program_id(0)
    live = i < nused_ref[0]
    new_expert = jnp.logical_or(i == 0, be_ref[i] != be_ref[jnp.maximum(i - 1, 0)])

    @pl.when(jnp.logical_and(live, new_expert))
    def _():
        wgb_ref[...] = wg_ref[...].astype(BF16)
        wub_ref[...] = wu_ref[...].astype(BF16)
        wdb_ref[...] = wd_ref[...].astype(BF16)

    @pl.when(live)
    def _():
        x = xs_ref[...]
        g = _dot(x, wgb_ref[...])
        u = _dot(x, wub_ref[...])
        act = g * (1.0 / (1.0 + jnp.exp(-g))) * u
        ys_ref[...] = (_dot(act.astype(BF16), wdb_ref[...]) * sw_ref[...]).astype(ys_ref.dtype)

    @pl.when(jnp.logical_not(live))
    def _():
        ys_ref[...] = jnp.zeros_like(ys_ref)


def _moe_experts(xs, slot_w, block_expert, n_used, wg, wu, wd, layer, *, n_slots, first_block,
                 ys_prev=None):
    rows, d = xs.shape
    ff = wg.shape[3]
    n_blocks = rows // MOE_BLOCK
    prev_specs = [] if ys_prev is None else [pl.BlockSpec(memory_space=pl.ANY)]
    prev_args = [] if ys_prev is None else [ys_prev]
    return pl.pallas_call(
        _moe_kernel if ys_prev is None else _moe_kernel_aliased,
        out_shape=jax.ShapeDtypeStruct((n_slots, d), BF16),
        grid_spec=pltpu.PrefetchScalarGridSpec(
            num_scalar_prefetch=2,
            grid=(n_blocks,),
            in_specs=[
                pl.BlockSpec((MOE_BLOCK, d), lambda i, be, nu: (i, 0)),
                pl.BlockSpec((MOE_BLOCK, 1), lambda i, be, nu: (i, 0)),
                pl.BlockSpec((None, None, d, ff), lambda i, be, nu: (layer, be[i], 0, 0)),
                pl.BlockSpec((None, None, d, ff), lambda i, be, nu: (layer, be[i], 0, 0)),
                pl.BlockSpec((None, None, ff, d), lambda i, be, nu: (layer, be[i], 0, 0)),
            ] + prev_specs,
            out_specs=pl.BlockSpec((MOE_BLOCK, d), lambda i, be, nu: (i + first_block, 0)),
            scratch_shapes=[pltpu.VMEM((d, ff), BF16), pltpu.VMEM((d, ff), BF16),
                            pltpu.VMEM((ff, d), BF16)],
        ),
        input_output_aliases={} if ys_prev is None else {7: 0},
        compiler_params=pltpu.CompilerParams(
            dimension_semantics=("arbitrary",),
            vmem_limit_bytes=52 * MIB),
        name="moe_experts",
    )(block_expert, n_used, xs, slot_w, wg, wu, wd, *prev_args)


def _dispatch_plan(expert, weights):
    t = expert.shape[0]
    n_assign = t * EXPERT_TOPK
    flat_e = expert.reshape(-1)
    iota = jnp.arange(n_assign, dtype=jnp.int32)
    _, order, w_sorted = lax.sort((flat_e, iota, weights.reshape(-1)), num_keys=1, is_stable=True)
    _, rank = lax.sort((order, iota), num_keys=1, is_stable=True)
    ids = jnp.arange(N_EXPERTS, dtype=jnp.int32)
    onehot = flat_e[:, None] == ids[None, :]
    counts = jnp.sum(onehot.astype(jnp.int32), axis=0)
    padded = ((counts + MOE_BLOCK - 1) // MOE_BLOCK) * MOE_BLOCK
    pend = jnp.cumsum(padded)
    pstart = pend - padded
    cstart = jnp.cumsum(counts) - counts
    n_blocks = -(-n_assign // MOE_BLOCK) + N_EXPERTS
    block_start = jnp.arange(n_blocks, dtype=jnp.int32) * MOE_BLOCK
    block_expert = jnp.minimum(jnp.sum((pend[None, :] <= block_start[:, None]).astype(jnp.int32), axis=1),
                               N_EXPERTS - 1)
    off = (block_start - pstart[block_expert])[:, None] + jnp.arange(MOE_BLOCK, dtype=jnp.int32)[None, :]
    pos = jnp.clip(cstart[block_expert][:, None] + off, 0, n_assign - 1).reshape(-1)
    slot_token = order[pos] // EXPERT_TOPK
    slot_w = w_sorted[pos][:, None]
    shift = jnp.sum(jnp.where(onehot, (pstart - cstart)[None, :], 0), axis=1)
    assign_slot = (rank + shift).reshape(t, EXPERT_TOPK)
    n_used = (pend[-1] // MOE_BLOCK).astype(jnp.int32).reshape(1)
    return slot_token, slot_w, assign_slot, block_expert.astype(jnp.int32), n_used


def _alibi_slopes(n_heads):
    return jnp.exp2(-8.0 * jnp.arange(1, n_heads + 1, dtype=F32) / n_heads)


def kernel(x, attn_norm_g, w_in, gate_bias, moba_q_gain, moba_k_gain, diff_q_gain, diff_k_gain,
           diff_lambda_q1, diff_lambda_k1, diff_lambda_q2, diff_lambda_k2, diff_head_gain,
           w_branch, w_out, ffn_norm_g, w_router_group, w_router_expert,
           w_expert_gate, w_expert_up, w_expert_down):
    b, s, d = x.shape
    depth = w_in.shape[0]
    mw = d // 2
    heads = mw // HEAD_DIM
    diff_heads = mw // (2 * HEAD_DIM)
    hb = mw // HEAD_DIM
    t = b * s
    tn = min(1024, mw)
    tm = min(1024, t)
    assert mw % tn == 0 and d % tn == 0 and t % tm == 0

    def seg_blocks(mixer, part):
        first = (mixer * 3 + part) * mw // tn
        return list(range(first, first + mw // tn))

    gate_blocks = list(range(9 * mw // tn, (9 * mw + N_BRANCH * d) // tn))
    qk_blocks = seg_blocks(0, 0) + seg_blocks(0, 1) + seg_blocks(1, 0) + seg_blocks(1, 1)
    rest_blocks = (seg_blocks(0, 2) + seg_blocks(1, 2) + seg_blocks(2, 0) + seg_blocks(2, 1)
                   + seg_blocks(2, 2))
    w_in_bf16 = w_in.astype(BF16)

    slopes_moba = _alibi_slopes(heads)
    slopes_diff = _alibi_slopes(diff_heads)
    w_branch_bf16 = w_branch.astype(BF16)
    w_out_bf16 = w_out.astype(BF16)

    def layer(xc, l):
        tc = xc.shape[0]
        bc = tc // s
        qk_gain = jnp.concatenate([jnp.tile(moba_q_gain[l], hb), jnp.tile(moba_k_gain[l], hb),
                                   jnp.tile(diff_q_gain[l], hb), jnp.tile(diff_k_gain[l], hb)])[None, :]
        g_attn = attn_norm_g[l][None, :]
        gates = _in_proj(xc, g_attn, w_in_bf16, l, gate_blocks, gate_bias[l].reshape(1, -1),
                         kind=KIND_GATE, tm=tm, tn=tn, name="in_proj_gate")
        qk = _in_proj(xc, g_attn, w_in_bf16, l, qk_blocks, qk_gain, kind=KIND_NORM, tm=tm, tn=tn,
                      name="in_proj_qk").reshape(bc, s, -1)
        rest = _in_proj(xc, g_attn, w_in_bf16, l, rest_blocks, jnp.ones((1, 5 * mw), F32),
                        kind=KIND_PLAIN, tm=tm, tn=tn, name="in_proj_rest").reshape(bc, s, -1)

        hp = min(HEADS_PER_STEP, heads)
        oa = _moba(qk, rest, slopes_moba, n_heads=heads, hp=hp, q_col=0, k_col=hb, v_col=0)

        lam_init = 0.8 - 0.6 * math.exp(-0.3 * l)
        lam = (jnp.exp(jnp.sum(diff_lambda_q1[l] * diff_lambda_k1[l]))
               - jnp.exp(jnp.sum(diff_lambda_q2[l] * diff_lambda_k2[l])) + lam_init)
        ob = _diff_attention(qk, rest, slopes_diff, lam.reshape(1).astype(F32), diff_head_gain[l][None, :],
                             n_heads=diff_heads, q_col=2 * hb, k_col=3 * hb, v_col2=hb // 2,
                             out_scale=1.0 - lam_init)

        oc = _sb_attention(rest, n_heads=heads, hp=hp, q_col=2 * hb, k_col=3 * hb,
                           v_col=4 * hb)

        w_router = jnp.concatenate([w_router_group[l], w_router_expert[l]], axis=1)
        w_router = jnp.pad(w_router, ((0, 0), (0, LANES - w_router.shape[1])))
        wr_hi, wr_lo = _split_bf16(w_router)
        x_mid, h2d, route = _merge(
            oa.reshape(tc, mw), ob.reshape(tc, mw), oc.reshape(tc, mw), gates, xc,
            w_branch_bf16[l], w_out_bf16[l], ffn_norm_g[l][None, :], wr_hi, wr_lo, tm=256)

        weights = route[:, 0:EXPERT_TOPK]
        expert = route[:, EXPERT_TOPK:2 * EXPERT_TOPK].astype(jnp.int32)
        slot_token, slot_w, assign_slot, block_expert, n_used = _dispatch_plan(expert, weights)
        n_slots = slot_token.shape[0]
        n_blocks = n_slots // MOE_BLOCK
        ys = None
        for first, last in ((0, n_blocks // 2), (n_blocks // 2, n_blocks)):
            rows = slice(first * MOE_BLOCK, last * MOE_BLOCK)
            ys = _moe_experts(h2d[slot_token[rows]], slot_w[rows], block_expert[first:last],
                              jnp.clip(n_used - first, 0, last - first),
                              w_expert_gate, w_expert_up, w_expert_down, l,
                              n_slots=n_slots, first_block=first, ys_prev=ys)
        y = ys[assign_slot[:, 0]].astype(F32) + ys[assign_slot[:, 1]].astype(F32)
        return x_mid + y

    n_chunks = BATCH_CHUNKS if b % BATCH_CHUNKS == 0 else 1
    tc = t // n_chunks
    tm = min(tm, tc)
    x2d = x.reshape(t, d)
    chunks = [x2d[c * tc:(c + 1) * tc] for c in range(n_chunks)]
    for l in range(depth):
        chunks = [layer(xc, l) for xc in chunks]
    return jnp.concatenate(chunks, axis=0).reshape(b, s, d)
```

```python
import functools
import math

import jax
import jax.numpy as jnp
from jax import lax
from jax.experimental import pallas as pl
from jax.experimental.pallas import tpu as pltpu

F32 = jnp.float32
BF16 = jnp.bfloat16

HEAD_DIM = 128
N_BRANCH = 3
MOBA_BLOCK = 256
MOBA_TOPK = 3
N_GROUPS = 4
EXPERTS_PER_GROUP = 8
N_EXPERTS = N_GROUPS * EXPERTS_PER_GROUP
EXPERT_TOPK = 2
MOE_BLOCK = 256
RMS_EPS = 1e-6
ATT_BLOCK = 256
ATT_TILE = 512
SB_TILE = 512
MOBA_TILE = 256
BATCH_CHUNKS = 1
DIFF_HEADS_PER_STEP = 2
HEADS_PER_STEP = 4
LANES = 128
BF16_SUBLANES = 16
MXU_DIM = 256
NEG_INF = float("-inf")
LOG2E = 1.4426950408889634
MIB = 1024 * 1024

KIND_PLAIN, KIND_NORM, KIND_GATE = 0, 1, 2


def _nt_dot(a, b):
    return lax.dot_general(a, b, (((1,), (1,)), ((), ())), preferred_element_type=F32)


def _dot(a, b):
    return jnp.dot(a, b, preferred_element_type=F32)


def _split_bf16(x):
    hi = x.astype(BF16)
    lo = (x - hi.astype(F32)).astype(BF16)
    return hi, lo


def _row_max(x):
    return jnp.max(x, axis=-1, keepdims=True)


def _row_sum(x):
    return jnp.sum(x, axis=-1, keepdims=True)


def _in_proj_kernel(src_ref, x_ref, g_ref, w_ref, vec_ref, o_ref, xn_ref, *, kind, chunk):
    del src_ref
    @pl.when(pl.program_id(1) == 0)
    def _():
        x = x_ref[...]
        ms = jnp.mean(x * x, axis=-1, keepdims=True)
        xn_ref[...] = (x * lax.rsqrt(ms + RMS_EPS) * g_ref[...]).astype(BF16)

    xn = xn_ref[...]
    for c in range(o_ref.shape[1] // chunk):
        cols = slice(c * chunk, (c + 1) * chunk)
        acc = _dot(xn, w_ref[:, cols])
        if kind == KIND_PLAIN:
            o_ref[:, cols] = acc.astype(o_ref.dtype)
        elif kind == KIND_GATE:
            z = acc + vec_ref[:, cols]
            o_ref[:, cols] = (0.5 * jnp.tanh(0.5 * z) + 0.5).astype(o_ref.dtype)
        else:
            for hd in range(chunk // HEAD_DIM):
                sub = slice(hd * HEAD_DIM, (hd + 1) * HEAD_DIM)
                hcols = slice(c * chunk + hd * HEAD_DIM, c * chunk + (hd + 1) * HEAD_DIM)
                blk = acc[:, sub]
                ms = jnp.mean(blk * blk, axis=-1, keepdims=True)
                o_ref[:, hcols] = (blk * lax.rsqrt(ms + RMS_EPS) * vec_ref[:, hcols]).astype(o_ref.dtype)


def _in_proj(x2d, g, w_all, layer, src_blocks, vec, *, kind, tm, tn, name):
    t, d = x2d.shape
    nj = len(src_blocks)
    kern = functools.partial(_in_proj_kernel, kind=kind, chunk=min(MXU_DIM, tn))
    return pl.pallas_call(
        kern,
        out_shape=jax.ShapeDtypeStruct((t, nj * tn), BF16),
        grid_spec=pltpu.PrefetchScalarGridSpec(
            num_scalar_prefetch=1,
            grid=(t // tm, nj),
            in_specs=[
                pl.BlockSpec((tm, d), lambda i, j, src: (i, 0)),
                pl.BlockSpec((1, d), lambda i, j, src: (0, 0)),
                pl.BlockSpec((None, d, tn), lambda i, j, src: (layer, 0, src[j])),
                pl.BlockSpec((1, tn), lambda i, j, src: (0, j)),
            ],
            out_specs=pl.BlockSpec((tm, tn), lambda i, j, src: (i, j)),
            scratch_shapes=[pltpu.VMEM((tm, d), BF16)],
        ),
        compiler_params=pltpu.CompilerParams(
            dimension_semantics=("parallel", "arbitrary"),
            vmem_limit_bytes=48 * MIB),
        name=name,
    )(jnp.asarray(src_blocks, jnp.int32), x2d, g, w_all, vec)


def _alibi_table(slopes, tq, s):
    r = jnp.arange(tq, dtype=jnp.int32)[:, None]
    j = jnp.arange(s, dtype=jnp.int32)[None, :]
    dist = (r - j + (s - tq)).astype(F32)
    return jnp.where(dist >= 0.0, (-LOG2E * slopes)[:, None, None] * dist[None], NEG_INF)


def _per_tile(qi, n_tiles, body):
    for c in range(n_tiles):
        pl.when(qi == c)(functools.partial(body, c))


def _head_cols(j, width=HEAD_DIM):
    return slice(j * width, (j + 1) * width)


def _moba_kernel(q_ref, qall_ref, k_ref, v_ref, bias_ref, o_ref, sel_ref,
                 *, hp, nb, blk, tq, topk, scale):
    qi = pl.program_id(2)
    s_len = nb * blk

    @pl.when(qi == 0)
    def _():
        nbp = -(-nb // BF16_SUBLANES) * BF16_SUBLANES
        row_id = lax.broadcasted_iota(jnp.int32, (nbp, HEAD_DIM), 0)
        blk_id = lax.broadcasted_iota(jnp.int32, (nbp, s_len), 0)
        own = lax.broadcasted_iota(jnp.int32, (nbp, s_len), 1) // blk
        blk_f = blk_id.astype(F32)
        for j in range(hp):
            hc = _head_cols(j)
            kmean = jnp.zeros((nbp, HEAD_DIM), F32)
            for n in range(nb):
                mean_n = jnp.sum(k_ref[n * blk:(n + 1) * blk, hc].astype(F32), axis=0, keepdims=True)
                kmean = jnp.where(row_id == n, mean_n * (1.0 / blk), kmean)
            km_hi, km_lo = _split_bf16(kmean)
            qall = qall_ref[:, hc]
            gate = _nt_dot(km_hi, qall) + _nt_dot(km_lo, qall)
            gate = jnp.where(blk_id < own, gate, NEG_INF)
            sel = jnp.where(blk_id == own, 1.0, 0.0)
            for _ in range(topk):
                gmax = jnp.max(gate, axis=0, keepdims=True)
                first = jnp.min(jnp.where(gate == gmax, blk_f, float(nbp)), axis=0, keepdims=True)
                pick = jnp.logical_and(blk_f == first, gmax > NEG_INF)
                sel = jnp.where(pick, 1.0, sel)
                gate = jnp.where(pick, NEG_INF, gate)
            mask = jnp.where(sel > 0.5, 0.0, NEG_INF)
            mask = jnp.concatenate([mask, jnp.full((LANES - nbp, s_len), NEG_INF, F32)], axis=0)
            sel_ref[j] = mask.T

    rows = pl.ds(pl.multiple_of(qi * tq, tq), tq)

    def tile(c):
        w = (c + 1) * tq
        for j in range(hp):
            hc = _head_cols(j)
            mask = sel_ref[j, rows, :]
            s = _nt_dot(q_ref[:, hc], k_ref[0:w, hc]) * scale + bias_ref[j, :, s_len - w:s_len]
            s = jnp.concatenate(
                [s[:, n * blk:(n + 1) * blk] + mask[:, n:n + 1] for n in range(w // blk)], axis=1)
            p = jnp.exp2(s - _row_max(s))
            o = _dot(p.astype(BF16), v_ref[0:w, hc]) / _row_sum(p)
            o_ref[:, hc] = o.astype(o_ref.dtype)

    _per_tile(qi, s_len // tq, tile)


def _moba(qk, rest, slopes, *, n_heads, hp, q_col, k_col, v_col):
    b, s, _ = qk.shape
    blk = MOBA_BLOCK
    tq = min(MOBA_TILE, s)
    nb = s // blk
    assert s % blk == 0 and nb <= LANES and tq % blk == 0 and s % tq == 0
    assert n_heads % hp == 0 and q_col % hp == 0 and k_col % hp == 0 and v_col % hp == 0
    bias = _alibi_table(slopes, tq, s)
    kern = functools.partial(_moba_kernel, hp=hp, nb=nb, blk=blk, tq=tq, topk=min(MOBA_TOPK, nb),
                             scale=LOG2E * HEAD_DIM ** -0.5)
    wd = hp * HEAD_DIM
    return pl.pallas_call(
        kern,
        out_shape=jax.ShapeDtypeStruct((b, s, n_heads * HEAD_DIM), BF16),
        grid=(b, n_heads // hp, s // tq),
        in_specs=[
            pl.BlockSpec((None, tq, wd), lambda bi, h, qi: (bi, qi, q_col // hp + h)),
            pl.BlockSpec((None, s, wd), lambda bi, h, qi: (bi, 0, q_col // hp + h)),
            pl.BlockSpec((None, s, wd), lambda bi, h, qi: (bi, 0, k_col // hp + h)),
            pl.BlockSpec((None, s, wd), lambda bi, h, qi: (bi, 0, v_col // hp + h)),
            pl.BlockSpec((hp, tq, s), lambda bi, h, qi: (h, 0, 0)),
        ],
        out_specs=pl.BlockSpec((None, tq, wd), lambda bi, h, qi: (bi, qi, h)),
        scratch_shapes=[pltpu.VMEM((hp, s, LANES), F32)],
        compiler_params=pltpu.CompilerParams(
            dimension_semantics=("parallel", "parallel", "arbitrary"),
            vmem_limit_bytes=48 * MIB),
        name="moba",
    )(qk, qk, qk, rest, bias)


def _diff_kernel(lam_ref, q_ref, k_ref, v_ref, bias_ref, gain_ref, o_ref,
                 *, hp, nq, tq, scale, out_scale):
    qi = pl.program_id(2)
    lam = lam_ref[0]
    s_len = nq * tq
    dv = 2 * HEAD_DIM

    def tile(c):
        w = (c + 1) * tq
        for j in range(hp):
            bias = bias_ref[j, :, s_len - w:s_len]
            v = v_ref[0:w, _head_cols(j, dv)]

            def attend(cols):
                s = _nt_dot(q_ref[:, cols], k_ref[0:w, cols]) * scale + bias
                p = jnp.exp2(s - _row_max(s))
                return _dot(p.astype(BF16), v) / _row_sum(p)

            o = attend(_head_cols(2 * j)) - lam * attend(_head_cols(2 * j + 1))
            ms = jnp.mean(o * o, axis=-1, keepdims=True)
            o_ref[:, _head_cols(j, dv)] = (o * lax.rsqrt(ms + RMS_EPS) * gain_ref[...]
                                           * out_scale).astype(o_ref.dtype)

    _per_tile(qi, nq, tile)


def _diff_attention(qk, rest, slopes, lam, head_gain, *, n_heads, hp, q_col2, k_col2, v_col2, out_scale):
    b, s, _ = qk.shape
    tq = min(ATT_TILE, s)
    nq = s // tq
    dv = 2 * HEAD_DIM
    assert n_heads % hp == 0 and q_col2 % hp == 0 and k_col2 % hp == 0 and v_col2 % hp == 0
    bias = _alibi_table(slopes, tq, s)
    kern = functools.partial(_diff_kernel, hp=hp, nq=nq, tq=tq, scale=LOG2E * HEAD_DIM ** -0.5,
                             out_scale=out_scale)
    wd = hp * dv
    return pl.pallas_call(
        kern,
        out_shape=jax.ShapeDtypeStruct((b, s, n_heads * dv), BF16),
        grid_spec=pltpu.PrefetchScalarGridSpec(
            num_scalar_prefetch=1,
            grid=(b, n_heads // hp, nq),
            in_specs=[
                pl.BlockSpec((None, tq, wd), lambda bi, h, qi, lm: (bi, qi, q_col2 // hp + h)),
                pl.BlockSpec((None, s, wd), lambda bi, h, qi, lm: (bi, 0, k_col2 // hp + h)),
                pl.BlockSpec((None, s, wd), lambda bi, h, qi, lm: (bi, 0, v_col2 // hp + h)),
                pl.BlockSpec((hp, tq, s), lambda bi, h, qi, lm: (h, 0, 0)),
                pl.BlockSpec((1, dv), lambda bi, h, qi, lm: (0, 0)),
            ],
            out_specs=pl.BlockSpec((None, tq, wd), lambda bi, h, qi, lm: (bi, qi, h)),
        ),
        compiler_params=pltpu.CompilerParams(
            dimension_semantics=("parallel", "parallel", "arbitrary"),
            vmem_limit_bytes=48 * MIB),
        name="diff_attn",
    )(lam, qk, qk, rest, bias, head_gain)


def _sb_kernel(q_ref, k_ref, v_ref, o_ref, *, hp, nq, tq, kb, scale):
    qi = pl.program_id(2)
    row = lax.broadcasted_iota(jnp.int32, (tq, kb), 0)
    col = lax.broadcasted_iota(jnp.int32, (tq, kb), 1)
    urow = lax.broadcasted_iota(jnp.int32, (kb, kb), 0)
    ucol = lax.broadcasted_iota(jnp.int32, (kb, kb), 1)
    upper = jnp.where(urow > ucol, 1.0, 0.0).astype(BF16)

    def tile(c):
        carry = [None] * hp
        out = [None] * hp
        first_diag = c * tq // kb
        for n in range((c + 1) * tq // kb - 1, -1, -1):
            ks = slice(n * kb, (n + 1) * kb)
            strict = (col + n * kb) < (row + c * tq)
            for j in range(hp):
                hc = _head_cols(j)
                nz = _nt_dot(q_ref[:, hc], k_ref[ks, hc]) * (-scale)
                log_keep = jnp.minimum(nz, 0.0) - jnp.log(1.0 + jnp.exp(-jnp.abs(nz)))
                if n >= first_diag:
                    log_keep = jnp.where(strict, log_keep, 0.0)
                after = _dot(log_keep.astype(BF16), upper)
                if carry[j] is not None:
                    after = after + carry[j]
                a = jnp.exp(log_keep - nz + after)
                if n >= first_diag:
                    a = jnp.where(strict, a, 0.0)
                pv = _dot(a.astype(BF16), v_ref[ks, hc])
                out[j] = pv if out[j] is None else out[j] + pv
                if n > 0:
                    rs = _row_sum(log_keep)
                    carry[j] = rs if carry[j] is None else carry[j] + rs
        for j in range(hp):
            o_ref[:, _head_cols(j)] = out[j].astype(o_ref.dtype)

    _per_tile(qi, nq, tile)


def _sb_attention(rest, *, n_heads, hp, q_col, k_col, v_col):
    b, s, _ = rest.shape
    tq = min(SB_TILE, s)
    kb = ATT_BLOCK
    nq = s // tq
    assert n_heads % hp == 0 and q_col % hp == 0 and k_col % hp == 0 and v_col % hp == 0
    assert s % tq == 0 and tq % kb == 0
    kern = functools.partial(_sb_kernel, hp=hp, nq=nq, tq=tq, kb=kb, scale=HEAD_DIM ** -0.5)
    wd = hp * HEAD_DIM
    return pl.pallas_call(
        kern,
        out_shape=jax.ShapeDtypeStruct((b, s, n_heads * HEAD_DIM), BF16),
        grid=(b, n_heads // hp, nq),
        in_specs=[
            pl.BlockSpec((None, tq, wd), lambda bi, h, qi: (bi, qi, q_col // hp + h)),
            pl.BlockSpec((None, s, wd), lambda bi, h, qi: (bi, 0, k_col // hp + h)),
            pl.BlockSpec((None, s, wd), lambda bi, h, qi: (bi, 0, v_col // hp + h)),
        ],
        out_specs=pl.BlockSpec((None, tq, wd), lambda bi, h, qi: (bi, qi, h)),
        compiler_params=pltpu.CompilerParams(
            dimension_semantics=("parallel", "parallel", "arbitrary"),
            vmem_limit_bytes=48 * MIB),
        name="stick_breaking",
    )(rest, rest, rest)


def _route_tile(lg):
    lane = lax.broadcasted_iota(jnp.int32, lg.shape, 1).astype(F32)
    first_of = lambda mask: jnp.min(jnp.where(mask, lane, float(LANES)), axis=-1, keepdims=True)

    is_group = lane < float(N_GROUPS)
    gl = jnp.where(is_group, lg, NEG_INF)
    ge = jnp.where(is_group, jnp.exp(gl - _row_max(gl)), 0.0)
    g_prob = ge / _row_sum(ge)
    g_top = _row_max(g_prob)
    g_idx = first_of(g_prob == g_top)

    e_lo = float(N_GROUPS) + float(EXPERTS_PER_GROUP) * g_idx
    in_group = jnp.logical_and(lane >= e_lo, lane < e_lo + float(EXPERTS_PER_GROUP))
    el = jnp.where(in_group, lg, NEG_INF)
    ee = jnp.where(in_group, jnp.exp(el - _row_max(el)), 0.0)
    e_prob = jnp.where(in_group, ee / _row_sum(ee), -1.0)
    top1 = _row_max(e_prob)
    idx1 = first_of(e_prob == top1)
    e_rest = jnp.where(lane == idx1, -1.0, e_prob)
    top2 = _row_max(e_rest)
    idx2 = first_of(e_rest == top2)
    denom = top1 + top2
    w1 = g_top * top1 / denom
    w2 = g_top * top2 / denom
    out = jnp.where(lane == 0.0, w1, 0.0)
    out = jnp.where(lane == 1.0, w2, out)
    out = jnp.where(lane == 2.0, idx1 - float(N_GROUPS), out)
    out = jnp.where(lane == 3.0, idx2 - float(N_GROUPS), out)
    return out


def _merge_kernel(oa_ref, ob_ref, oc_ref, ga_ref, gb_ref, gc_ref, x_ref, wb_ref, wo_ref, fg_ref,
                  wr_hi_ref, wr_lo_ref, xo_ref, h_ref, route_ref):
    mixed = ga_ref[...].astype(F32) * _dot(oa_ref[...], wb_ref[0])
    mixed += gb_ref[...].astype(F32) * _dot(ob_ref[...], wb_ref[1])
    mixed += gc_ref[...].astype(F32) * _dot(oc_ref[...], wb_ref[2])
    xn = x_ref[...] + _dot(mixed.astype(BF16), wo_ref[...])
    xo_ref[...] = xn
    ms = jnp.mean(xn * xn, axis=-1, keepdims=True)
    hn = xn * lax.rsqrt(ms + RMS_EPS) * fg_ref[...]
    h_hi, h_lo = _split_bf16(hn)
    h_ref[...] = h_hi
    logits = (_dot(h_hi, wr_hi_ref[...]) + _dot(h_hi, wr_lo_ref[...]) + _dot(h_lo, wr_hi_ref[...]))
    route_ref[...] = _route_tile(logits)


def _merge(oa, ob, oc, gates, x2d, wb, wo, fg, wr_hi, wr_lo, *, tm):
    t, d = x2d.shape
    mw = oa.shape[1]
    nr = wr_hi.shape[1]
    const = lambda *shape: pl.BlockSpec(shape, lambda i: (0,) * len(shape),
                                        pipeline_mode=pl.Buffered(1))
    gspec = lambda g: pl.BlockSpec((tm, d), lambda i: (i, g))
    return pl.pallas_call(
        _merge_kernel,
        out_shape=(jax.ShapeDtypeStruct((t, d), F32),
                   jax.ShapeDtypeStruct((t, d), BF16),
                   jax.ShapeDtypeStruct((t, nr), F32)),
        grid=(t // tm,),
        in_specs=[
            pl.BlockSpec((tm, mw), lambda i: (i, 0)),
            pl.BlockSpec((tm, mw), lambda i: (i, 0)),
            pl.BlockSpec((tm, mw), lambda i: (i, 0)),
            gspec(0), gspec(1), gspec(2),
            pl.BlockSpec((tm, d), lambda i: (i, 0)),
            const(N_BRANCH, mw, d),
            const(d, d),
            const(1, d),
            const(d, nr),
            const(d, nr),
        ],
        out_specs=(pl.BlockSpec((tm, d), lambda i: (i, 0)),
                   pl.BlockSpec((tm, d), lambda i: (i, 0)),
                   pl.BlockSpec((tm, nr), lambda i: (i, 0))),
        compiler_params=pltpu.CompilerParams(
            dimension_semantics=("parallel",),
            vmem_limit_bytes=56 * MIB),
        name="merge_out",
    )(oa, ob, oc, gates, gates, gates, x2d, wb, wo, fg, wr_hi, wr_lo)


def _moe_kernel_aliased(be_ref, nused_ref, xs_ref, sw_ref, wg_ref, wu_ref, wd_ref, prev_ref, *refs):
    del prev_ref
    _moe_kernel(be_ref, nused_ref, xs_ref, sw_ref, wg_ref, wu_ref, wd_ref, *refs)


def _moe_kernel(be_ref, nused_ref, xs_ref, sw_ref, wg_ref, wu_ref, wd_ref, ys_ref,
                wgb_ref, wub_ref, wdb_ref):
    i = pl.program_id(0)
    live = i < nused_ref[0]
    new_expert = jnp.logical_or(i == 0, be_ref[i] != be_ref[jnp.maximum(i - 1, 0)])

    @pl.when(jnp.logical_and(live, new_expert))
    def _():
        wgb_ref[...] = wg_ref[...].astype(BF16)
        wub_ref[...] = wu_ref[...].astype(BF16)
        wdb_ref[...] = wd_ref[...].astype(BF16)

    @pl.when(live)
    def _():
        x = xs_ref[...]
        g = _dot(x, wgb_ref[...])
        u = _dot(x, wub_ref[...])
        act = g * (1.0 / (1.0 + jnp.exp(-g))) * u
        ys_ref[...] = (_dot(act.astype(BF16), wdb_ref[...]) * sw_ref[...]).astype(ys_ref.dtype)

    @pl.when(jnp.logical_not(live))
    def _():
        ys_ref[...] = jnp.zeros_like(ys_ref)


def _moe_experts(xs, slot_w, block_expert, n_used, wg, wu, wd, layer, *, n_slots, first_block,
                 ys_prev=None):
    rows, d = xs.shape
    ff = wg.shape[3]
    n_blocks = rows // MOE_BLOCK
    prev_specs = [] if ys_prev is None else [pl.BlockSpec(memory_space=pl.ANY)]
    prev_args = [] if ys_prev is None else [ys_prev]
    return pl.pallas_call(
        _moe_kernel if ys_prev is None else _moe_kernel_aliased,
        out_shape=jax.ShapeDtypeStruct((n_slots, d), BF16),
        grid_spec=pltpu.PrefetchScalarGridSpec(
            num_scalar_prefetch=2,
            grid=(n_blocks,),
            in_specs=[
                pl.BlockSpec((MOE_BLOCK, d), lambda i, be, nu: (i, 0)),
                pl.BlockSpec((MOE_BLOCK, 1), lambda i, be, nu: (i, 0)),
                pl.BlockSpec((None, None, d, ff), lambda i, be, nu: (layer, be[i], 0, 0)),
                pl.BlockSpec((None, None, d, ff), lambda i, be, nu: (layer, be[i], 0, 0)),
                pl.BlockSpec((None, None, ff, d), lambda i, be, nu: (layer, be[i], 0, 0)),
            ] + prev_specs,
            out_specs=pl.BlockSpec((MOE_BLOCK, d), lambda i, be, nu: (i + first_block, 0)),
            scratch_shapes=[pltpu.VMEM((d, ff), BF16), pltpu.VMEM((d, ff), BF16),
                            pltpu.VMEM((ff, d), BF16)],
        ),
        input_output_aliases={} if ys_prev is None else {7: 0},
        compiler_params=pltpu.CompilerParams(
            dimension_semantics=("arbitrary",),
            vmem_limit_bytes=52 * MIB),
        name="moe_experts",
    )(block_expert, n_used, xs, slot_w, wg, wu, wd, *prev_args)


def _dispatch_plan(expert, weights):
    t = expert.shape[0]
    n_assign = t * EXPERT_TOPK
    flat_e = expert.reshape(-1)
    iota = jnp.arange(n_assign, dtype=jnp.int32)
    _, order, w_sorted = lax.sort((flat_e, iota, weights.reshape(-1)), num_keys=1, is_stable=True)
    _, rank = lax.sort((order, iota), num_keys=1, is_stable=True)
    ids = jnp.arange(N_EXPERTS, dtype=jnp.int32)
    onehot = flat_e[:, None] == ids[None, :]
    counts = jnp.sum(onehot.astype(jnp.int32), axis=0)
    padded = ((counts + MOE_BLOCK - 1) // MOE_BLOCK) * MOE_BLOCK
    pend = jnp.cumsum(padded)
    pstart = pend - padded
    cstart = jnp.cumsum(counts) - counts
    n_blocks = -(-n_assign // MOE_BLOCK) + N_EXPERTS
    block_start = jnp.arange(n_blocks, dtype=jnp.int32) * MOE_BLOCK
    block_expert = jnp.minimum(jnp.sum((pend[None, :] <= block_start[:, None]).astype(jnp.int32), axis=1),
                               N_EXPERTS - 1)
    off = (block_start - pstart[block_expert])[:, None] + jnp.arange(MOE_BLOCK, dtype=jnp.int32)[None, :]
    pos = jnp.clip(cstart[block_expert][:, None] + off, 0, n_assign - 1).reshape(-1)
    slot_token = order[pos] // EXPERT_TOPK
    slot_w = w_sorted[pos][:, None]
    shift = jnp.sum(jnp.where(onehot, (pstart - cstart)[None, :], 0), axis=1)
    assign_slot = (rank + shift).reshape(t, EXPERT_TOPK)
    n_used = (pend[-1] // MOE_BLOCK).astype(jnp.int32).reshape(1)
    return slot_token, slot_w, assign_slot, block_expert.astype(jnp.int32), n_used


def _alibi_slopes(n_heads):
    return jnp.exp2(-8.0 * jnp.arange(1, n_heads + 1, dtype=F32) / n_heads)


def kernel(x, attn_norm_g, w_in, gate_bias, moba_q_gain, moba_k_gain, diff_q_gain, diff_k_gain,
           diff_lambda_q1, diff_lambda_k1, diff_lambda_q2, diff_lambda_k2, diff_head_gain,
           w_branch, w_out, ffn_norm_g, w_router_group, w_router_expert,
           w_expert_gate, w_expert_up, w_expert_down):
    b, s, d = x.shape
    depth = w_in.shape[0]
    mw = d // 2
    heads = mw // HEAD_DIM
    diff_heads = mw // (2 * HEAD_DIM)
    hb = mw // HEAD_DIM
    t = b * s
    tn = min(1024, mw)
    tm = min(1024, t)
    assert mw % tn == 0 and d % tn == 0 and t % tm == 0

    def seg_blocks(mixer, part):
        first = (mixer * 3 + part) * mw // tn
        return list(range(first, first + mw // tn))

    gate_blocks = list(range(9 * mw // tn, (9 * mw + N_BRANCH * d) // tn))
    qk_blocks = seg_blocks(0, 0) + seg_blocks(0, 1) + seg_blocks(1, 0) + seg_blocks(1, 1)
    rest_blocks = (seg_blocks(0, 2) + seg_blocks(1, 2) + seg_blocks(2, 0) + seg_blocks(2, 1)
                   + seg_blocks(2, 2))
    w_in_bf16 = w_in.astype(BF16)

    slopes_moba = _alibi_slopes(heads)
    slopes_diff = _alibi_slopes(diff_heads)
    w_branch_bf16 = w_branch.astype(BF16)
    w_out_bf16 = w_out.astype(BF16)

    def layer(xc, l):
        tc = xc.shape[0]
        bc = tc // s
        qk_gain = jnp.concatenate([jnp.tile(moba_q_gain[l], hb), jnp.tile(moba_k_gain[l], hb),
                                   jnp.tile(diff_q_gain[l], hb), jnp.tile(diff_k_gain[l], hb)])[None, :]
        g_attn = attn_norm_g[l][None, :]
        gates = _in_proj(xc, g_attn, w_in_bf16, l, gate_blocks, gate_bias[l].reshape(1, -1),
                         kind=KIND_GATE, tm=tm, tn=tn, name="in_proj_gate")
        qk = _in_proj(xc, g_attn, w_in_bf16, l, qk_blocks, qk_gain, kind=KIND_NORM, tm=tm, tn=tn,
                      name="in_proj_qk").reshape(bc, s, -1)
        rest = _in_proj(xc, g_attn, w_in_bf16, l, rest_blocks, jnp.ones((1, 5 * mw), F32),
                        kind=KIND_PLAIN, tm=tm, tn=tn, name="in_proj_rest").reshape(bc, s, -1)

        hp = min(HEADS_PER_STEP, heads)
        oa = _moba(qk, rest, slopes_moba, n_heads=heads, hp=hp, q_col=0, k_col=hb, v_col=0)

        lam_init = 0.8 - 0.6 * math.exp(-0.3 * l)
        lam = (jnp.exp(jnp.sum(diff_lambda_q1[l] * diff_lambda_k1[l]))
               - jnp.exp(jnp.sum(diff_lambda_q2[l] * diff_lambda_k2[l])) + lam_init)
        ob = _diff_attention(qk, rest, slopes_diff, lam.reshape(1).astype(F32), diff_head_gain[l][None, :],
                             n_heads=diff_heads, hp=min(DIFF_HEADS_PER_STEP, diff_heads),
                             q_col2=hb, k_col2=3 * hb // 2, v_col2=hb // 2, out_scale=1.0 - lam_init)

        oc = _sb_attention(rest, n_heads=heads, hp=hp, q_col=2 * hb, k_col=3 * hb,
                           v_col=4 * hb)

        w_router = jnp.concatenate([w_router_group[l], w_router_expert[l]], axis=1)
        w_router = jnp.pad(w_router, ((0, 0), (0, LANES - w_router.shape[1])))
        wr_hi, wr_lo = _split_bf16(w_router)
        x_mid, h2d, route = _merge(
            oa.reshape(tc, mw), ob.reshape(tc, mw), oc.reshape(tc, mw), gates, xc,
            w_branch_bf16[l], w_out_bf16[l], ffn_norm_g[l][None, :], wr_hi, wr_lo, tm=256)

        weights = route[:, 0:EXPERT_TOPK]
        expert = route[:, EXPERT_TOPK:2 * EXPERT_TOPK].astype(jnp.int32)
        slot_token, slot_w, assign_slot, block_expert, n_used = _dispatch_plan(expert, weights)
        n_slots = slot_token.shape[0]
        n_blocks = n_slots // MOE_BLOCK
        ys = None
        for first, last in ((0, n_blocks // 2), (n_blocks // 2, n_blocks)):
            rows = slice(first * MOE_BLOCK, last * MOE_BLOCK)
            ys = _moe_experts(h2d[slot_token[rows]], slot_w[rows], block_expert[first:last],
                              jnp.clip(n_used - first, 0, last - first),
                              w_expert_gate, w_expert_up, w_expert_down, l,
                              n_slots=n_slots, first_block=first, ys_prev=ys)
        y = ys[assign_slot[:, 0]].astype(F32) + ys[assign_slot[:, 1]].astype(F32)
        return x_mid + y

    n_chunks = BATCH_CHUNKS if b % BATCH_CHUNKS == 0 else 1
    tc = t // n_chunks
    tm = min(tm, tc)
    x2d = x.reshape(t, d)
    chunks = [x2d[c * tc:(c + 1) * tc] for c in range(n_chunks)]
    for l in range(depth):
        chunks = [layer(xc, l) for xc in chunks]
    return jnp.concatenate(chunks, axis=0).reshape(b, s, d)
```
